```python
import math
import jax, jax.numpy as jnp
from jax import lax
import numpy as np

D_MODEL = 4096
BATCH = 2
SEQ = 4096
DEPTH = 2

MIX_WIDTH = D_MODEL
SSM_WIDTH = D_MODEL // 2
RET_WIDTH = MIX_WIDTH - SSM_WIDTH
SSM_GROUP = 16
SSM_GROUPS = SSM_WIDTH // SSM_GROUP
SSM_STATE = 64
RET_HEAD_DIM = 256
RET_HEADS = RET_WIDTH // RET_HEAD_DIM
RET_CHUNK = 128
D_FF = 4 * D_MODEL
IN_WIDTH = SSM_WIDTH + 4 * RET_WIDTH
ROPE_BASE = 10000.0
EPS = 1e-6
DT_MIN = 1e-3
DT_MAX = 1e-1

kernel_name = "hymba_s5_retnet_adaln_trunk"


def rmsnorm(x, g):
    xf = x.astype(jnp.float32)
    y = xf * lax.rsqrt(jnp.mean(jnp.square(xf), axis=-1, keepdims=True) + EPS)
    return (y * g.astype(jnp.float32)).astype(x.dtype)


def rotate(t, positions):
    dh = t.shape[-1]
    inv_freq = ROPE_BASE ** (-jnp.arange(0, dh, 2, dtype=jnp.float32) / dh)
    ang = positions.astype(jnp.float32)[..., None] * inv_freq
    cos = jnp.cos(ang)[:, :, None, :]
    sin = jnp.sin(ang)[:, :, None, :]
    t1, t2 = jnp.split(t, 2, axis=-1)
    return jnp.concatenate([t1 * cos - t2 * sin, t1 * sin + t2 * cos], axis=-1)


def s5_branch(u, lam_re, lam_im, log_dt, b_re, b_im, c_re, c_im, d_skip, w_glu, b_glu, g_out):
    f32 = jnp.float32
    bsz, seq, _ = u.shape
    uf = u.astype(f32).reshape(bsz, seq, SSM_GROUPS, SSM_GROUP)
    lam = lax.complex(lam_re.astype(f32), lam_im.astype(f32))
    dt = jnp.exp(log_dt.astype(f32))[:, None]
    lam_bar = jnp.exp(lam * dt)
    b_mat = lax.complex(b_re.astype(f32), b_im.astype(f32))
    b_bar = ((lam_bar - 1.0) / lam)[..., None] * b_mat
    c_mat = lax.complex(c_re.astype(f32), c_im.astype(f32))
    bu = jnp.einsum('bsgh,gph->bsgp', uf, b_bar)
    a = jnp.broadcast_to(lam_bar, bu.shape)

    def combine(left, right):
        a_l, b_l = left
        a_r, b_r = right
        return a_r * a_l, a_r * b_l + b_r

    _, states = lax.associative_scan(combine, (a, bu), axis=1)
    y = jnp.einsum('bsgp,ghp->bsgh', states, c_mat).real \
        + d_skip.astype(f32).reshape(SSM_GROUPS, SSM_GROUP) * uf
    y = jax.nn.gelu(y.reshape(bsz, seq, SSM_WIDTH))
    y = y * jax.nn.sigmoid(y @ w_glu.astype(f32) + b_glu.astype(f32))
    return rmsnorm(y, g_out)


def retention_branch(q, k, v, g, positions, g_norm):
    f32 = jnp.float32
    bsz, seq, _ = q.shape
    H, dh = RET_HEADS, RET_HEAD_DIM
    q = rotate(q.astype(f32).reshape(bsz, seq, H, dh), positions)
    k = rotate(k.astype(f32).reshape(bsz, seq, H, dh), positions) * (dh ** -0.5)
    v = v.astype(f32).reshape(bsz, seq, H, dh)

    log_gamma = jnp.log1p(-jnp.exp2(-5.0 - jnp.arange(H, dtype=f32)))
    idx = jnp.arange(RET_CHUNK, dtype=f32)
    rel = idx[:, None] - idx[None, :]
    decay_mask = jnp.where(rel >= 0,
                           jnp.exp(log_gamma[:, None, None] * jnp.maximum(rel, 0.0)),
                           0.0)
    cross_decay = jnp.exp(log_gamma[:, None] * (idx + 1.0))
    state_decay = jnp.exp(log_gamma[:, None] * (RET_CHUNK - 1.0 - idx))
    chunk_decay = jnp.exp(log_gamma * RET_CHUNK)
    n_chunks = seq // RET_CHUNK

    def to_chunks(t):
        return t.reshape(bsz, n_chunks, RET_CHUNK, H, dh).transpose(1, 0, 3, 2, 4)

    def step(state, qkv):
        qc, kc, vc = qkv
        scores = jnp.einsum('bhnd,bhmd->bhnm', qc, kc) * decay_mask
        inner = jnp.einsum('bhnm,bhme->bhne', scores, vc)
        cross = jnp.einsum('bhnd,bhde->bhne', qc, state) * cross_decay[None, :, :, None]
        new_state = state * chunk_decay[None, :, None, None] + jnp.einsum(
            'bhmd,bhme->bhde', kc * state_decay[None, :, :, None], vc)
        return new_state, inner + cross

    state0 = jnp.zeros((bsz, H, dh, dh), f32)
    _, out = lax.scan(step, state0, (to_chunks(q), to_chunks(k), to_chunks(v)))
    out = out.transpose(1, 0, 3, 2, 4).reshape(bsz, seq, H, dh)
    mu = jnp.mean(out, axis=-1, keepdims=True)
    var = jnp.mean(jnp.square(out - mu), axis=-1, keepdims=True)
    out = ((out - mu) * lax.rsqrt(var + EPS)).reshape(bsz, seq, RET_WIDTH)
    out = out * g_norm.astype(f32)
    return out * jax.nn.silu(g.astype(f32))


def setup_inputs(seed: int = 0) -> dict:
    key = jax.random.key(seed)
    ks = jax.random.split(key, 24)
    f32 = jnp.float32
    L, D, G, P, Hc = DEPTH, D_MODEL, SSM_GROUPS, SSM_STATE, SSM_GROUP

    def nrm(k, shape, scale):
        return jax.random.normal(k, shape, f32) * scale

    x = nrm(ks[0], (BATCH, SEQ, D), 1.0)
    c = nrm(ks[1], (BATCH, D), 1.0)
    offsets = jax.random.randint(ks[2], (BATCH, 1), 0, 1024, dtype=jnp.int32)
    positions = offsets + jnp.arange(SEQ, dtype=jnp.int32)[None, :]

    w_ada = nrm(ks[3], (L, D, 6 * D), 0.5 * D ** -0.5)
    b_ada = nrm(ks[4], (L, 6 * D), 0.02)
    g_mix = 1.0 + nrm(ks[5], (L, D), 0.02)
    w_in = nrm(ks[6], (L, D, IN_WIDTH), D ** -0.5)
    n_idx = jnp.arange(P, dtype=f32)
    lam_re = -0.5 + nrm(ks[7], (L, G, P), 0.01)
    lam_im = math.pi * n_idx + nrm(ks[8], (L, G, P), 0.01)
    log_dt = jax.random.uniform(ks[9], (L, G), f32, math.log(DT_MIN), math.log(DT_MAX))
    b_re = nrm(ks[10], (L, G, P, Hc), (2.0 * Hc) ** -0.5)
    b_im = nrm(ks[11], (L, G, P, Hc), (2.0 * Hc) ** -0.5)
    c_re = nrm(ks[12], (L, G, Hc, P), (2.0 * P) ** -0.5)
    c_im = nrm(ks[13], (L, G, Hc, P), (2.0 * P) ** -0.5)
    d_skip = nrm(ks[14], (L, SSM_WIDTH), 1.0)
    w_glu = nrm(ks[15], (L, SSM_WIDTH, SSM_WIDTH), SSM_WIDTH ** -0.5)
    b_glu = nrm(ks[16], (L, SSM_WIDTH), 0.02)
    g_ssm_out = 1.0 + nrm(ks[17], (L, SSM_WIDTH), 0.02)
    g_ret_norm = 1.0 + nrm(ks[18], (L, RET_WIDTH), 0.02)
    w_out = nrm(ks[19], (L, MIX_WIDTH, D), MIX_WIDTH ** -0.5)
    g_mlp = 1.0 + nrm(ks[20], (L, D), 0.02)
    w_up = nrm(ks[21], (L, D, D_FF), D ** -0.5)
    w_down = nrm(ks[22], (L, D_FF, D), D_FF ** -0.5)
    g_final = 1.0 + nrm(ks[23], (D,), 0.02)
    return {"x": x, "c": c, "positions": positions, "w_ada": w_ada, "b_ada": b_ada,
            "g_mix": g_mix, "w_in": w_in, "lam_re": lam_re, "lam_im": lam_im,
            "log_dt": log_dt, "b_re": b_re, "b_im": b_im, "c_re": c_re, "c_im": c_im,
            "d_skip": d_skip, "w_glu": w_glu, "b_glu": b_glu, "g_ssm_out": g_ssm_out,
            "g_ret_norm": g_ret_norm, "w_out": w_out, "g_mlp": g_mlp, "w_up": w_up,
            "w_down": w_down, "g_final": g_final}


def reference(x, c, positions, w_ada, b_ada, g_mix, w_in, lam_re, lam_im, log_dt,
              b_re, b_im, c_re, c_im, d_skip, w_glu, b_glu, g_ssm_out, g_ret_norm,
              w_out, g_mlp, w_up, w_down, g_final):
    split_points = [SSM_WIDTH + i * RET_WIDTH for i in range(4)]
    c_act = jax.nn.silu(c)
    for i in range(DEPTH):
        mod = (c_act @ w_ada[i] + b_ada[i])[:, None, :]
        shift1, scale1, gate1, shift2, scale2, gate2 = jnp.split(mod, 6, axis=-1)

        h = rmsnorm(x, g_mix[i]) * (1.0 + scale1) + shift1
        proj = h @ w_in[i]
        u, q, k, v, g = jnp.split(proj, split_points, axis=-1)
        ssm_out = s5_branch(u, lam_re[i], lam_im[i], log_dt[i], b_re[i], b_im[i],
                            c_re[i], c_im[i], d_skip[i], w_glu[i], b_glu[i], g_ssm_out[i])
        ret_out = retention_branch(q, k, v, g, positions, g_ret_norm[i])
        mixed = jnp.concatenate([ssm_out.astype(ret_out.dtype), ret_out], axis=-1)
        mixed = mixed.astype(x.dtype) @ w_out[i]
        x = x + gate1 * mixed

        h = rmsnorm(x, g_mlp[i]) * (1.0 + scale2) + shift2
        x = x + gate2 * (jnp.square(jax.nn.relu(h @ w_up[i])) @ w_down[i])
    return rmsnorm(x, g_final)
```

```python
import functools

import numpy as np
import jax
import jax.numpy as jnp
from jax import lax
from jax.experimental import pallas as pl
from jax.experimental.pallas import tpu as pltpu

F32 = jnp.float32
BF16 = jnp.bfloat16

SSM_WIDTH = 2048
RET_WIDTH = 2048
SSM_GROUP = 16
SSM_STATE = 64
RET_HEAD_DIM = 256
RET_HEADS = 8
RET_CHUNK = 128
ROPE_BASE = 10000.0
EPS = 1e-6

V7X_LANES = 128
V7X_VMEM_BYTES = 64 * 1024 * 1024
V7X_COMPILER_SCRATCH_BYTES = 6 * 1024 * 1024

S5_CHUNK = 128
S5_GROUPS_PER_TILE = V7X_LANES // SSM_GROUP
S5_TILE_STATES = S5_GROUPS_PER_TILE * SSM_STATE


def _nbytes(shape, dtype):
    return int(np.prod(shape)) * jnp.dtype(dtype).itemsize


def _params(semantics, pipelined_blocks, scratch_blocks=()):
    need = 2 * sum(_nbytes(s, d) for s, d in pipelined_blocks)
    need += sum(_nbytes(s, d) for s, d in scratch_blocks)
    limit = min(need + V7X_COMPILER_SCRATCH_BYTES, V7X_VMEM_BYTES - 4 * 1024 * 1024)
    return pltpu.CompilerParams(dimension_semantics=semantics, vmem_limit_bytes=limit)


def _ada_kernel(c_ref, w_ref, b_ref, o_ref):
    c = c_ref[...]
    c_act = c * jax.nn.sigmoid(c)
    o_ref[...] = jnp.dot(c_act.astype(BF16), w_ref[...].astype(BF16),
                         preferred_element_type=F32) + b_ref[...]


def _ada_mod(c, w_ada, b_ada, *, tn=512):
    depth, d, n = w_ada.shape
    bsz = c.shape[0]
    rows = 8
    c_pad = jnp.zeros((rows, d), F32).at[:bsz].set(c)
    blocks = [((rows, d), F32), ((d, tn), F32), ((1, tn), F32), ((rows, tn), F32)]
    out = pl.pallas_call(
        _ada_kernel,
        grid=(depth, n // tn),
        in_specs=[pl.BlockSpec((rows, d), lambda l, j: (0, 0)),
                  pl.BlockSpec((None, d, tn), lambda l, j: (l, 0, j)),
                  pl.BlockSpec((None, 1, tn), lambda l, j: (l, 0, j))],
        out_specs=pl.BlockSpec((None, rows, tn), lambda l, j: (l, 0, j)),
        out_shape=jax.ShapeDtypeStruct((depth, rows, n), F32),
        compiler_params=_params(("parallel", "parallel"), blocks),
        name="ada_mod",
    )(c_pad, w_ada, b_ada.reshape(depth, 1, n))
    return out[:, :bsz]


def _normmod_kernel(x_ref, g_ref, sc_ref, sh_ref, o_ref):
    x = x_ref[...]
    ms = jnp.mean(x * x, axis=-1, keepdims=True)
    y = x * lax.rsqrt(ms + EPS) * g_ref[...]
    o_ref[...] = (y * (1.0 + sc_ref[...]) + sh_ref[...]).astype(o_ref.dtype)


def _normmod(x2, g, scale, shift, seq, *, tm=256):
    t, d = x2.shape
    bsz = scale.shape[0]
    blocks = [((tm, d), F32), ((1, d), F32), ((1, d), F32), ((1, d), F32), ((tm, d), BF16)]
    per_b = pl.BlockSpec((None, 1, d), lambda i: (i * tm // seq, 0, 0))
    return pl.pallas_call(
        _normmod_kernel,
        grid=(t // tm,),
        in_specs=[pl.BlockSpec((tm, d), lambda i: (i, 0)),
                  pl.BlockSpec((1, d), lambda i: (0, 0)), per_b, per_b],
        out_specs=pl.BlockSpec((tm, d), lambda i: (i, 0)),
        out_shape=jax.ShapeDtypeStruct((t, d), BF16),
        compiler_params=_params(("parallel",), blocks),
        name="normmod",
    )(x2, g.reshape(1, d), scale.reshape(bsz, 1, d), shift.reshape(bsz, 1, d))


def _rmsnorm_kernel(x_ref, g_ref, o_ref):
    x = x_ref[...]
    ms = jnp.mean(x * x, axis=-1, keepdims=True)
    o_ref[...] = x * lax.rsqrt(ms + EPS) * g_ref[...]


def _rmsnorm(x2, g, *, tm=256):
    t, d = x2.shape
    blocks = [((tm, d), F32), ((1, d), F32), ((tm, d), F32)]
    return pl.pallas_call(
        _rmsnorm_kernel,
        grid=(t // tm,),
        in_specs=[pl.BlockSpec((tm, d), lambda i: (i, 0)),
                  pl.BlockSpec((1, d), lambda i: (0, 0))],
        out_specs=pl.BlockSpec((tm, d), lambda i: (i, 0)),
        out_shape=jax.ShapeDtypeStruct((t, d), F32),
        compiler_params=_params(("parallel",), blocks),
        name="final_rmsnorm",
    )(x2, g.reshape(1, d))


def _epi_none(acc):
    return acc


def _epi_relu2(acc):
    r = jnp.maximum(acc, 0.0)
    return r * r


def _epi_resid(acc, x_ref, gate_ref):
    return x_ref[...] + gate_ref[...] * acc


def _mm_kernel(a_ref, w_ref, *refs, nk, epilogue):
    if nk == 1:
        *extra, o_ref = refs
        prod = jnp.dot(a_ref[...], w_ref[...], preferred_element_type=F32)
        o_ref[...] = epilogue(prod, *extra).astype(o_ref.dtype)
        return
    *extra, o_ref, acc_ref = refs
    k = pl.program_id(2)
    prod = jnp.dot(a_ref[...], w_ref[...], preferred_element_type=F32)

    @pl.when(k == 0)
    def _():
        acc_ref[...] = prod

    @pl.when(jnp.logical_and(k > 0, k < nk - 1))
    def _():
        acc_ref[...] += prod

    @pl.when(k == nk - 1)
    def _():
        o_ref[...] = epilogue(acc_ref[...] + prod, *extra).astype(o_ref.dtype)


def _matmul(a, w, *, tm, tn, tk, out_dtype, epilogue=_epi_none, resid=None, gate=None,
            seq=None, name):
    m, kdim = a.shape
    _, n = w.shape
    nk = kdim // tk
    in_specs = [pl.BlockSpec((tm, tk), lambda i, j, k: (i, k)),
                pl.BlockSpec((tk, tn), lambda i, j, k: (k, j))]
    blocks = [((tm, tk), a.dtype), ((tk, tn), w.dtype), ((tm, tn), out_dtype)]
    args = [a, w]
    if resid is not None:
        bsz = gate.shape[0]
        in_specs += [pl.BlockSpec((tm, tn), lambda i, j, k: (i, j)),
                     pl.BlockSpec((None, 1, tn), lambda i, j, k: (i * tm // seq, 0, j))]
        blocks += [((tm, tn), F32), ((1, tn), F32)]
        args += [resid, gate.reshape(bsz, 1, n)]
    scratch = [] if nk == 1 else [pltpu.VMEM((tm, tn), F32)]
    scratch_blocks = [((tm, tn), F32)] * (1 if nk == 1 else 2)
    return pl.pallas_call(
        functools.partial(_mm_kernel, nk=nk, epilogue=epilogue),
        grid=(m // tm, n // tn, nk),
        in_specs=in_specs,
        out_specs=pl.BlockSpec((tm, tn), lambda i, j, k: (i, j)),
        out_shape=jax.ShapeDtypeStruct((m, n), out_dtype),
        scratch_shapes=scratch,
        compiler_params=_params(("parallel", "parallel", "arbitrary"), blocks, scratch_blocks),
        name=name,
    )(*args)


def _rope_kernel(pos_ref, inv_ref, cos_ref, sin_ref):
    ang = pos_ref[...].astype(F32) * inv_ref[...]
    cos_ref[...] = jnp.cos(ang)
    sin_ref[...] = jnp.sin(ang)


def _rope_tables(positions, *, tm=1024):
    t = positions.size
    half = RET_HEAD_DIM // 2
    inv = ROPE_BASE ** (-np.arange(0, RET_HEAD_DIM, 2, dtype=np.float64) / RET_HEAD_DIM)
    inv = jnp.asarray(inv.astype(np.float32)).reshape(1, half)
    blocks = [((tm, 1), jnp.int32), ((1, half), F32), ((tm, half), F32), ((tm, half), F32)]
    return pl.pallas_call(
        _rope_kernel,
        grid=(t // tm,),
        in_specs=[pl.BlockSpec((tm, 1), lambda i: (i, 0)),
                  pl.BlockSpec((1, half), lambda i: (0, 0))],
        out_specs=[pl.BlockSpec((tm, half), lambda i: (i, 0))] * 2,
        out_shape=[jax.ShapeDtypeStruct((t, half), F32)] * 2,
        compiler_params=_params(("parallel",), blocks),
        name="rope_tables",
    )(positions.reshape(t, 1), inv)


def _s5_prep_kernel(lr_ref, li_ref, ldt_ref, br_ref, bi_ref, cr_ref, ci_ref,
                    bblk_ref, cblk_ref, pn_ref, pp_ref):
    w = S5_TILE_STATES
    lr = lr_ref[...]
    li = li_ref[...]
    dt = jnp.exp(ldt_ref[...])
    a = lr * dt
    th = li * dt
    mag = jnp.exp(a)
    lbr = mag * jnp.cos(th)
    lbi = mag * jnp.sin(th)
    den = lr * lr + li * li
    nr = lbr - 1.0
    coef_r = (nr * lr + lbi * li) / den
    coef_i = (lbi * lr - nr * li) / den
    br = br_ref[...]
    bi = bi_ref[...]
    bbar_r = coef_r * br - coef_i * bi
    bbar_i = coef_r * bi + coef_i * br

    rows = lax.broadcasted_iota(jnp.int32, (V7X_LANES, w), 0)
    cols = lax.broadcasted_iota(jnp.int32, (V7X_LANES, w), 1)
    same_group = ((rows >> (SSM_GROUP.bit_length() - 1))
                  == (cols >> (SSM_STATE.bit_length() - 1)))
    reps = (S5_GROUPS_PER_TILE, 1)

    def blockdiag(v):
        return jnp.where(same_group, jnp.tile(v, reps), 0.0)

    bblk_ref[:, :w] = blockdiag(bbar_r).astype(bblk_ref.dtype)
    bblk_ref[:, w:] = blockdiag(bbar_i).astype(bblk_ref.dtype)
    cblk_ref[:, :w] = blockdiag(cr_ref[...]).astype(cblk_ref.dtype)
    cblk_ref[:, w:] = blockdiag(-ci_ref[...]).astype(cblk_ref.dtype)

    m = (lax.broadcasted_iota(jnp.int32, (S5_CHUNK, w), 0) + 1).astype(F32)
    cs = jnp.cos(m * th)
    sn = jnp.sin(m * th)
    grow = jnp.exp(m * a)
    shrink = jnp.exp(-(m * a))
    pp_ref[:, :w] = grow * cs
    pp_ref[:, w:] = grow * sn
    pn_ref[:, :w] = shrink * cs
    pn_ref[:, w:] = -(shrink * sn)


def _s5_prep(lam_re, lam_im, log_dt, b_re, b_im, c_re, c_im):
    g, p, h = b_re.shape
    gl, w = S5_GROUPS_PER_TILE, S5_TILE_STATES
    nt = g // gl
    row = lambda v: v.reshape(nt, 1, w)
    bt = lambda v: v.reshape(nt, gl, p, h).transpose(0, 3, 1, 2).reshape(nt, h, w)
    ct = lambda v: v.reshape(nt, gl, h, p).transpose(0, 2, 1, 3).reshape(nt, h, w)
    row_spec = pl.BlockSpec((None, 1, w), lambda j: (j, 0, 0))
    mat_spec = pl.BlockSpec((None, h, w), lambda j: (j, 0, 0))
    blk_spec = pl.BlockSpec((None, V7X_LANES, 2 * w), lambda j: (j, 0, 0))
    pow_spec = pl.BlockSpec((None, S5_CHUNK, 2 * w), lambda j: (j, 0, 0))
    blocks = ([((1, w), F32)] * 3 + [((h, w), F32)] * 4 + [((V7X_LANES, 2 * w), BF16)] * 2
              + [((S5_CHUNK, 2 * w), F32)] * 2)
    return pl.pallas_call(
        _s5_prep_kernel,
        grid=(nt,),
        in_specs=[row_spec] * 3 + [mat_spec] * 4,
        out_specs=[blk_spec, blk_spec, pow_spec, pow_spec],
        out_shape=[jax.ShapeDtypeStruct((nt, V7X_LANES, 2 * w), BF16)] * 2
        + [jax.ShapeDtypeStruct((nt, S5_CHUNK, 2 * w), F32)] * 2,
        compiler_params=_params(("parallel",), blocks),
        name="s5_prep",
    )(row(lam_re), row(lam_im), row(jnp.repeat(log_dt, p)), bt(b_re), bt(b_im), ct(c_re), ct(c_im))


def _s5_kernel(u_ref, bblk_ref, cblk_ref, pn_ref, pp_ref, d_ref, y_ref, *, n_chunks):
    w = S5_TILE_STATES
    lc = S5_CHUNK
    bblk = bblk_ref[...]
    cblk = cblk_ref[...]
    d = d_ref[...]
    r = lax.broadcasted_iota(jnp.int32, (lc, lc), 0)
    c = lax.broadcasted_iota(jnp.int32, (lc, lc), 1)
    tri = (r >= c).astype(BF16)

    def chunk(t, carry):
        cr, ci = carry
        rows = pl.ds(pl.multiple_of(t * lc, lc), lc)
        u = u_ref[rows, :]
        bu = jnp.dot(u.astype(BF16), bblk, preferred_element_type=F32)
        br, bi = bu[:, :w], bu[:, w:]
        pnr, pni = pn_ref[:, :w], pn_ref[:, w:]
        z = jnp.concatenate([pnr * br - pni * bi, pnr * bi + pni * br], axis=1)
        ws = jnp.dot(tri, z.astype(BF16), preferred_element_type=F32)
        wr = ws[:, :w] + cr
        wi = ws[:, w:] + ci
        ppr, ppi = pp_ref[:, :w], pp_ref[:, w:]
        xr = ppr * wr - ppi * wi
        xi = ppr * wi + ppi * wr
        x = jnp.concatenate([xr, xi], axis=1).astype(BF16)
        y = lax.dot_general(x, cblk, (((1,), (1,)), ((), ())), preferred_element_type=F32)
        y_ref[rows, :] = jax.nn.gelu(y + d * u)
        return xr[lc - 1:, :], xi[lc - 1:, :]

    zero = jnp.zeros((1, w), F32)
    lax.fori_loop(0, n_chunks, chunk, (zero, zero))


def _s5_scan(proj, tables, d_skip, bsz, seq):
    bblk, cblk, pn, pp = tables
    nt = bblk.shape[0]
    w2 = 2 * S5_TILE_STATES
    tab = lambda rows: pl.BlockSpec((None, rows, w2), lambda j, b: (j, 0, 0))
    blocks = [((seq, V7X_LANES), F32), ((V7X_LANES, w2), BF16), ((V7X_LANES, w2), BF16),
              ((S5_CHUNK, w2), F32), ((S5_CHUNK, w2), F32), ((1, V7X_LANES), F32),
              ((seq, V7X_LANES), F32)]
    return pl.pallas_call(
        functools.partial(_s5_kernel, n_chunks=seq // S5_CHUNK),
        grid=(nt, bsz),
        in_specs=[pl.BlockSpec((seq, V7X_LANES), lambda j, b: (b, j)),
                  tab(V7X_LANES), tab(V7X_LANES), tab(S5_CHUNK), tab(S5_CHUNK),
                  pl.BlockSpec((1, V7X_LANES), lambda j, b: (0, j))],
        out_specs=pl.BlockSpec((seq, V7X_LANES), lambda j, b: (b, j)),
        out_shape=jax.ShapeDtypeStruct((bsz * seq, SSM_WIDTH), F32),
        compiler_params=_params(("parallel", "parallel"), blocks),
        name="s5_scan",
    )(proj, bblk, cblk, pn, pp, d_skip.reshape(1, SSM_WIDTH))


def _glu_kernel(y_ref, w_ref, b_ref, g_ref, o_ref):
    y = y_ref[...]
    z = jnp.dot(y.astype(BF16), w_ref[...], preferred_element_type=F32) + b_ref[...]
    v = y * jax.nn.sigmoid(z)
    ms = jnp.mean(v * v, axis=-1, keepdims=True)
    o_ref[...] = (v * lax.rsqrt(ms + EPS) * g_ref[...]).astype(o_ref.dtype)


def _glu_norm(y, w_glu, b_glu, g_out, *, tm=512):
    t, n = y.shape
    blocks = [((tm, n), F32), ((n, n), BF16), ((1, n), F32), ((1, n), F32), ((tm, n), BF16)]
    vec = pl.BlockSpec((1, n), lambda i: (0, 0))
    return pl.pallas_call(
        _glu_kernel,
        grid=(t // tm,),
        in_specs=[pl.BlockSpec((tm, n), lambda i: (i, 0)),
                  pl.BlockSpec((n, n), lambda i: (0, 0)), vec, vec],
        out_specs=pl.BlockSpec((tm, n), lambda i: (i, 0)),
        out_shape=jax.ShapeDtypeStruct((t, n), BF16),
        compiler_params=_params(("parallel",), blocks),
        name="glu_norm",
    )(y, w_glu, b_glu.reshape(1, n), g_out.reshape(1, n))


def _retention_tables():
    h, c, dh = RET_HEADS, RET_CHUNK, RET_HEAD_DIM
    log_gamma = np.log1p(-np.exp2(-5.0 - np.arange(h, dtype=np.float64)))
    idx = np.arange(c, dtype=np.float64)
    rel = idx[:, None] - idx[None, :]
    mask = np.where(rel >= 0, np.exp(log_gamma[:, None, None] * np.maximum(rel, 0.0)), 0.0)
    cross = np.exp(log_gamma[:, None] * (idx + 1.0))
    state = np.exp(log_gamma[:, None] * (c - 1.0 - idx))
    chunk = np.exp(log_gamma * c)
    wide = lambda v: np.broadcast_to(v[:, :, None], (h, c, dh))
    f = lambda v: jnp.asarray(np.ascontiguousarray(v).astype(np.float32))
    return f(mask), f(wide(cross)), f(wide(state)), f(np.broadcast_to(chunk[:, None, None], (h, 1, dh)))


def _retention_kernel(q_ref, k_ref, v_ref, g_ref, cos_ref, sin_ref, mask_ref, cd_ref, sd_ref,
                      ch_ref, gn_ref, o_ref, state_ref, *, chunks_per_step):
    half = RET_HEAD_DIM // 2
    c = RET_CHUNK

    @pl.when(pl.program_id(2) == 0)
    def _():
        state_ref[...] = jnp.zeros_like(state_ref)

    mask = mask_ref[...]
    cross_decay = cd_ref[...]
    state_decay = sd_ref[...]
    chunk_decay = ch_ref[...]
    gn = gn_ref[...]
    k_scale = RET_HEAD_DIM ** -0.5

    def rotate(t, cos, sin):
        t1, t2 = t[:, :half], t[:, half:]
        return jnp.concatenate([t1 * cos - t2 * sin, t1 * sin + t2 * cos], axis=1)

    for ci in range(chunks_per_step):
        rows = pl.ds(ci * c, c)
        cos = cos_ref[rows, :]
        sin = sin_ref[rows, :]
        q = rotate(q_ref[rows, :], cos, sin)
        k = rotate(k_ref[rows, :], cos, sin) * k_scale
        v = v_ref[rows, :].astype(BF16)
        qb = q.astype(BF16)
        state = state_ref[...]
        scores = lax.dot_general(qb, k.astype(BF16), (((1,), (1,)), ((), ())),
                                 preferred_element_type=F32) * mask
        inner = jnp.dot(scores.astype(BF16), v, preferred_element_type=F32)
        cross = jnp.dot(qb, state.astype(BF16), preferred_element_type=F32) * cross_decay
        kv = lax.dot_general((k * state_decay).astype(BF16), v, (((0,), (0,)), ((), ())),
                             preferred_element_type=F32)
        state_ref[...] = state * chunk_decay + kv
        out = inner + cross
        mu = jnp.mean(out, axis=-1, keepdims=True)
        dev = out - mu
        var = jnp.mean(dev * dev, axis=-1, keepdims=True)
        g = g_ref[rows, :]
        o_ref[rows, :] = (dev * lax.rsqrt(var + EPS) * gn * (g * jax.nn.sigmoid(g))).astype(o_ref.dtype)


def _retention(proj, cos, sin, g_norm, bsz, seq, *, chunks_per_step=4):
    dh, h = RET_HEAD_DIM, RET_HEADS
    rb = chunks_per_step * RET_CHUNK
    steps = seq // rb
    mask, cd, sd, ch = _retention_tables()
    first = SSM_WIDTH // dh
    col = lambda which: pl.BlockSpec(
        (rb, dh), lambda b, hh, s, which=which: (b * steps + s, first + which * h + hh))
    rope = pl.BlockSpec((rb, dh // 2), lambda b, hh, s: (b * steps + s, 0))
    per_head = lambda rows: pl.BlockSpec((None, rows, dh), lambda b, hh, s: (hh, 0, 0))
    blocks = ([((rb, dh), F32)] * 4 + [((rb, dh // 2), F32)] * 2
              + [((RET_CHUNK, RET_CHUNK), F32), ((RET_CHUNK, dh), F32), ((RET_CHUNK, dh), F32),
                 ((1, dh), F32), ((1, dh), F32), ((rb, dh), BF16)])
    return pl.pallas_call(
        functools.partial(_retention_kernel, chunks_per_step=chunks_per_step),
        grid=(bsz, h, steps),
        in_specs=[col(0), col(1), col(2), col(3), rope, rope,
                  pl.BlockSpec((None, RET_CHUNK, RET_CHUNK), lambda b, hh, s: (hh, 0, 0)),
                  per_head(RET_CHUNK), per_head(RET_CHUNK), per_head(1),
                  pl.BlockSpec((1, dh), lambda b, hh, s: (0, hh))],
        out_specs=pl.BlockSpec((rb, dh), lambda b, hh, s: (b * steps + s, hh)),
        out_shape=jax.ShapeDtypeStruct((bsz * seq, RET_WIDTH), BF16),
        scratch_shapes=[pltpu.VMEM((dh, dh), F32)],
        compiler_params=_params(("parallel", "parallel", "arbitrary"), blocks, [((dh, dh), F32)]),
        name="retention",
    )(proj, proj, proj, proj, cos, sin, mask, cd, sd, ch, g_norm.reshape(1, RET_WIDTH))


def kernel(x, c, positions, w_ada, b_ada, g_mix, w_in, lam_re, lam_im, log_dt, b_re, b_im, c_re,
           c_im, d_skip, w_glu, b_glu, g_ssm_out, g_ret_norm, w_out, g_mlp, w_up, w_down, g_final):
    bsz, seq, d = x.shape
    depth = w_ada.shape[0]
    x2 = x.reshape(bsz * seq, d)
    mod = _ada_mod(c, w_ada, b_ada)
    cos, sin = _rope_tables(positions)
    for i in range(depth):
        shift1, scale1, gate1, shift2, scale2, gate2 = jnp.split(mod[i], 6, axis=-1)

        h = _normmod(x2, g_mix[i], scale1, shift1, seq)
        proj = _matmul(h, w_in[i].astype(BF16), tm=1024, tn=1024, tk=d, out_dtype=F32,
                       name="in_proj")
        tables = _s5_prep(lam_re[i], lam_im[i], log_dt[i], b_re[i], b_im[i], c_re[i], c_im[i])
        y = _s5_scan(proj, tables, d_skip[i], bsz, seq)
        ssm_out = _glu_norm(y, w_glu[i].astype(BF16), b_glu[i], g_ssm_out[i])
        ret_out = _retention(proj, cos, sin, g_ret_norm[i], bsz, seq)
        mixed = jnp.concatenate([ssm_out, ret_out], axis=-1)
        x2 = _matmul(mixed, w_out[i].astype(BF16), tm=1024, tn=1024, tk=d, out_dtype=F32,
                     epilogue=_epi_resid, resid=x2, gate=gate1, seq=seq, name="out_proj")

        h = _normmod(x2, g_mlp[i], scale2, shift2, seq)
        a = _matmul(h, w_up[i].astype(BF16), tm=1024, tn=1024, tk=d, out_dtype=BF16,
                    epilogue=_epi_relu2, name="mlp_up")
        x2 = _matmul(a, w_down[i].astype(BF16), tm=1024, tn=1024, tk=2048, out_dtype=F32,
                     epilogue=_epi_resid, resid=x2, gate=gate2, seq=seq, name="mlp_down")
    return _rmsnorm(x2, g_final).reshape(bsz, seq, d)
```

```python
import functools

import numpy as np
import jax
import jax.numpy as jnp
from jax import lax
from jax.experimental import pallas as pl
from jax.experimental.pallas import tpu as pltpu

F32 = jnp.float32
BF16 = jnp.bfloat16

SSM_WIDTH = 2048
RET_WIDTH = 2048
SSM_GROUP = 16
SSM_STATE = 64
RET_HEAD_DIM = 256
RET_HEADS = 8
RET_CHUNK = 128
ROPE_BASE = 10000.0
EPS = 1e-6

V7X_LANES = 128
V7X_VMEM_BYTES = 64 * 1024 * 1024
V7X_COMPILER_SCRATCH_BYTES = 6 * 1024 * 1024

S5_CHUNK = 256
S5_GROUPS_PER_TILE = V7X_LANES // SSM_GROUP
S5_TILE_STATES = S5_GROUPS_PER_TILE * SSM_STATE


def _nbytes(shape, dtype):
    return int(np.prod(shape)) * jnp.dtype(dtype).itemsize


def _params(semantics, pipelined_blocks, scratch_blocks=()):
    need = 2 * sum(_nbytes(s, d) for s, d in pipelined_blocks)
    need += sum(_nbytes(s, d) for s, d in scratch_blocks)
    limit = min(need + V7X_COMPILER_SCRATCH_BYTES, V7X_VMEM_BYTES - 4 * 1024 * 1024)
    return pltpu.CompilerParams(dimension_semantics=semantics, vmem_limit_bytes=limit)


def _ada_kernel(c_ref, w_ref, b_ref, o_ref):
    c = c_ref[...]
    c_act = c * jax.nn.sigmoid(c)
    o_ref[...] = jnp.dot(c_act.astype(BF16), w_ref[...].astype(BF16),
                         preferred_element_type=F32) + b_ref[...]


def _ada_mod(c, w_ada, b_ada, *, tn=512):
    depth, d, n = w_ada.shape
    bsz = c.shape[0]
    rows = 8
    c_pad = jnp.zeros((rows, d), F32).at[:bsz].set(c)
    blocks = [((rows, d), F32), ((d, tn), F32), ((1, tn), F32), ((rows, tn), F32)]
    out = pl.pallas_call(
        _ada_kernel,
        grid=(depth, n // tn),
        in_specs=[pl.BlockSpec((rows, d), lambda l, j: (0, 0)),
                  pl.BlockSpec((None, d, tn), lambda l, j: (l, 0, j)),
                  pl.BlockSpec((None, 1, tn), lambda l, j: (l, 0, j))],
        out_specs=pl.BlockSpec((None, rows, tn), lambda l, j: (l, 0, j)),
        out_shape=jax.ShapeDtypeStruct((depth, rows, n), F32),
        compiler_params=_params(("parallel", "parallel"), blocks),
        name="ada_mod",
    )(c_pad, w_ada, b_ada.reshape(depth, 1, n))
    return out[:, :bsz]


def _normmod_kernel(x_ref, g_ref, sc_ref, sh_ref, o_ref):
    x = x_ref[...]
    ms = jnp.mean(x * x, axis=-1, keepdims=True)
    y = x * lax.rsqrt(ms + EPS) * g_ref[...]
    o_ref[...] = (y * (1.0 + sc_ref[...]) + sh_ref[...]).astype(o_ref.dtype)


def _normmod(x2, g, scale, shift, seq, *, tm=256):
    t, d = x2.shape
    bsz = scale.shape[0]
    blocks = [((tm, d), F32), ((1, d), F32), ((1, d), F32), ((1, d), F32), ((tm, d), BF16)]
    per_b = pl.BlockSpec((None, 1, d), lambda i: (i * tm // seq, 0, 0))
    return pl.pallas_call(
        _normmod_kernel,
        grid=(t // tm,),
        in_specs=[pl.BlockSpec((tm, d), lambda i: (i, 0)),
                  pl.BlockSpec((1, d), lambda i: (0, 0)), per_b, per_b],
        out_specs=pl.BlockSpec((tm, d), lambda i: (i, 0)),
        out_shape=jax.ShapeDtypeStruct((t, d), BF16),
        compiler_params=_params(("parallel",), blocks),
        name="normmod",
    )(x2, g.reshape(1, d), scale.reshape(bsz, 1, d), shift.reshape(bsz, 1, d))


def _rmsnorm_kernel(x_ref, g_ref, o_ref):
    x = x_ref[...]
    ms = jnp.mean(x * x, axis=-1, keepdims=True)
    o_ref[...] = x * lax.rsqrt(ms + EPS) * g_ref[...]


def _rmsnorm(x2, g, *, tm=256):
    t, d = x2.shape
    blocks = [((tm, d), F32), ((1, d), F32), ((tm, d), F32)]
    return pl.pallas_call(
        _rmsnorm_kernel,
        grid=(t // tm,),
        in_specs=[pl.BlockSpec((tm, d), lambda i: (i, 0)),
                  pl.BlockSpec((1, d), lambda i: (0, 0))],
        out_specs=pl.BlockSpec((tm, d), lambda i: (i, 0)),
        out_shape=jax.ShapeDtypeStruct((t, d), F32),
        compiler_params=_params(("parallel",), blocks),
        name="final_rmsnorm",
    )(x2, g.reshape(1, d))


def _epi_none(acc):
    return acc


def _epi_relu2(acc):
    r = jnp.maximum(acc, 0.0)
    return r * r


def _epi_resid(acc, x_ref, gate_ref):
    return x_ref[...] + gate_ref[...] * acc


def _mm_kernel(*refs, n_a, nk, epilogue):
    a_refs, w_ref, rest = refs[:n_a], refs[n_a], refs[n_a + 1:]

    def product():
        total, row = None, 0
        for a_ref in a_refs:
            kk = a_ref.shape[1]
            part = jnp.dot(a_ref[...], w_ref[row:row + kk, :], preferred_element_type=F32)
            total = part if total is None else total + part
            row += kk
        return total

    if nk == 1:
        *extra, o_ref = rest
        o_ref[...] = epilogue(product(), *extra).astype(o_ref.dtype)
        return
    *extra, o_ref, acc_ref = rest
    k = pl.program_id(2)

    @pl.when(k == 0)
    def _():
        acc_ref[...] = jnp.zeros_like(acc_ref)

    acc_ref[...] += product()

    @pl.when(k == nk - 1)
    def _():
        o_ref[...] = epilogue(acc_ref[...], *extra).astype(o_ref.dtype)


def _matmul(a_list, w, layer, *, tm, tn, tk, out_dtype, epilogue=_epi_none, resid=None,
            gate=None, seq=None, name):
    m = a_list[0].shape[0]
    _, kdim, n = w.shape
    nk = kdim // tk
    if len(a_list) == 1:
        in_specs = [pl.BlockSpec((tm, tk), lambda i, j, k: (i, k))]
        blocks = [((tm, tk), a_list[0].dtype)]
    else:
        assert nk == 1 and sum(a.shape[1] for a in a_list) == kdim
        in_specs = [pl.BlockSpec((tm, a.shape[1]), lambda i, j, k: (i, 0)) for a in a_list]
        blocks = [((tm, a.shape[1]), a.dtype) for a in a_list]
    in_specs.append(pl.BlockSpec((None, tk, tn), lambda i, j, k: (layer, k, j)))
    blocks += [((tk, tn), w.dtype), ((tm, tn), out_dtype)]
    args = list(a_list) + [w]
    if resid is not None:
        bsz = gate.shape[0]
        in_specs += [pl.BlockSpec((tm, tn), lambda i, j, k: (i, j)),
                     pl.BlockSpec((None, 1, tn), lambda i, j, k: (i * tm // seq, 0, j))]
        blocks += [((tm, tn), F32), ((1, tn), F32)]
        args += [resid, gate.reshape(bsz, 1, n)]
    scratch = [] if nk == 1 else [pltpu.VMEM((tm, tn), F32)]
    scratch_blocks = [((tm, tn), F32)] * (1 if nk == 1 else 2)
    return pl.pallas_call(
        functools.partial(_mm_kernel, n_a=len(a_list), nk=nk, epilogue=epilogue),
        grid=(m // tm, n // tn, nk),
        in_specs=in_specs,
        out_specs=pl.BlockSpec((tm, tn), lambda i, j, k: (i, j)),
        out_shape=jax.ShapeDtypeStruct((m, n), out_dtype),
        scratch_shapes=scratch,
        compiler_params=_params(("parallel", "parallel", "arbitrary"), blocks, scratch_blocks),
        name=name,
    )(*args)


def _rope_kernel(pos_ref, inv_ref, cos_ref, sin_ref):
    ang = pos_ref[...].astype(F32) * inv_ref[...]
    cos_ref[...] = jnp.cos(ang)
    sin_ref[...] = jnp.sin(ang)


def _rope_tables(positions, *, tm=1024):
    t = positions.size
    half = RET_HEAD_DIM // 2
    inv = ROPE_BASE ** (-np.arange(0, RET_HEAD_DIM, 2, dtype=np.float64) / RET_HEAD_DIM)
    inv = jnp.asarray(inv.astype(np.float32)).reshape(1, half)
    blocks = [((tm, 1), jnp.int32), ((1, half), F32), ((tm, half), F32), ((tm, half), F32)]
    return pl.pallas_call(
        _rope_kernel,
        grid=(t // tm,),
        in_specs=[pl.BlockSpec((tm, 1), lambda i: (i, 0)),
                  pl.BlockSpec((1, half), lambda i: (0, 0))],
        out_specs=[pl.BlockSpec((tm, half), lambda i: (i, 0))] * 2,
        out_shape=[jax.ShapeDtypeStruct((t, half), F32)] * 2,
        compiler_params=_params(("parallel",), blocks),
        name="rope_tables",
    )(positions.reshape(t, 1), inv)


def _s5_prep_kernel(lr_ref, li_ref, ldt_ref, br_ref, bi_ref, cr_ref, ci_ref,
                    bblk_ref, cblk_ref, pn_ref, pp_ref):
    w = S5_TILE_STATES
    lr = lr_ref[...]
    li = li_ref[...]
    dt = jnp.exp(ldt_ref[...])
    a = lr * dt
    th = li * dt
    mag = jnp.exp(a)
    lbr = mag * jnp.cos(th)
    lbi = mag * jnp.sin(th)
    den = lr * lr + li * li
    nr = lbr - 1.0
    coef_r = (nr * lr + lbi * li) / den
    coef_i = (lbi * lr - nr * li) / den
    br = br_ref[...]
    bi = bi_ref[...]
    bbar_r = coef_r * br - coef_i * bi
    bbar_i = coef_r * bi + coef_i * br

    rows = lax.broadcasted_iota(jnp.int32, (V7X_LANES, w), 0)
    cols = lax.broadcasted_iota(jnp.int32, (V7X_LANES, w), 1)
    same_group = ((rows >> (SSM_GROUP.bit_length() - 1))
                  == (cols >> (SSM_STATE.bit_length() - 1)))
    reps = (S5_GROUPS_PER_TILE, 1)

    def blockdiag(v):
        return jnp.where(same_group, jnp.tile(v, reps), 0.0)

    bblk_ref[:, :w] = blockdiag(bbar_r).astype(bblk_ref.dtype)
    bblk_ref[:, w:] = blockdiag(bbar_i).astype(bblk_ref.dtype)
    cblk_ref[:, :w] = blockdiag(cr_ref[...]).astype(cblk_ref.dtype)
    cblk_ref[:, w:] = blockdiag(-ci_ref[...]).astype(cblk_ref.dtype)

    m = (lax.broadcasted_iota(jnp.int32, (S5_CHUNK, w), 0) + 1).astype(F32)
    cs = jnp.cos(m * th)
    sn = jnp.sin(m * th)
    grow = jnp.exp(m * a)
    shrink = jnp.exp(-(m * a))
    pp_ref[:, :w] = grow * cs
    pp_ref[:, w:] = grow * sn
    pn_ref[:, :w] = shrink * cs
    pn_ref[:, w:] = -(shrink * sn)


def _s5_prep(lam_re, lam_im, log_dt, b_re, b_im, c_re, c_im):
    g, p, h = b_re.shape
    gl, w = S5_GROUPS_PER_TILE, S5_TILE_STATES
    nt = g // gl
    row = lambda v: v.reshape(nt, 1, w)
    bt = lambda v: v.reshape(nt, gl, p, h).transpose(0, 3, 1, 2).reshape(nt, h, w)
    ct = lambda v: v.reshape(nt, gl, h, p).transpose(0, 2, 1, 3).reshape(nt, h, w)
    row_spec = pl.BlockSpec((None, 1, w), lambda j: (j, 0, 0))
    mat_spec = pl.BlockSpec((None, h, w), lambda j: (j, 0, 0))
    blk_spec = pl.BlockSpec((None, V7X_LANES, 2 * w), lambda j: (j, 0, 0))
    pow_spec = pl.BlockSpec((None, S5_CHUNK, 2 * w), lambda j: (j, 0, 0))
    blocks = ([((1, w), F32)] * 3 + [((h, w), F32)] * 4 + [((V7X_LANES, 2 * w), BF16)] * 2
              + [((S5_CHUNK, 2 * w), F32)] * 2)
    return pl.pallas_call(
        _s5_prep_kernel,
        grid=(nt,),
        in_specs=[row_spec] * 3 + [mat_spec] * 4,
        out_specs=[blk_spec, blk_spec, pow_spec, pow_spec],
        out_shape=[jax.ShapeDtypeStruct((nt, V7X_LANES, 2 * w), BF16)] * 2
        + [jax.ShapeDtypeStruct((nt, S5_CHUNK, 2 * w), F32)] * 2,
        compiler_params=_params(("parallel",), blocks),
        name="s5_prep",
    )(row(lam_re), row(lam_im), row(jnp.repeat(log_dt, p)), bt(b_re), bt(b_im), ct(c_re), ct(c_im))


def _s5_kernel(u_ref, bblk_ref, cblk_ref, pn_ref, pp_ref, d_ref, y_ref, *, n_chunks):
    w = S5_TILE_STATES
    lc = S5_CHUNK
    bblk = bblk_ref[...]
    cblk = cblk_ref[...]
    d = d_ref[...]
    r = lax.broadcasted_iota(jnp.int32, (lc, lc), 0)
    c = lax.broadcasted_iota(jnp.int32, (lc, lc), 1)
    tri = (r >= c).astype(BF16)

    def chunk(t, carry):
        cr, ci = carry
        rows = pl.ds(pl.multiple_of(t * lc, lc), lc)
        u = u_ref[rows, :]
        bu = jnp.dot(u.astype(BF16), bblk, preferred_element_type=F32)
        br, bi = bu[:, :w], bu[:, w:]
        pnr, pni = pn_ref[:, :w], pn_ref[:, w:]
        z = jnp.concatenate([pnr * br - pni * bi, pnr * bi + pni * br], axis=1)
        ws = jnp.dot(tri, z.astype(BF16), preferred_element_type=F32)
        wr = ws[:, :w] + cr
        wi = ws[:, w:] + ci
        ppr, ppi = pp_ref[:, :w], pp_ref[:, w:]
        xr = ppr * wr - ppi * wi
        xi = ppr * wi + ppi * wr
        x = jnp.concatenate([xr, xi], axis=1).astype(BF16)
        y = lax.dot_general(x, cblk, (((1,), (1,)), ((), ())), preferred_element_type=F32)
        y_ref[rows, :] = jax.nn.gelu(y + d * u)
        return xr[lc - 1:, :], xi[lc - 1:, :]

    zero = jnp.zeros((1, w), F32)
    lax.fori_loop(0, n_chunks, chunk, (zero, zero), unroll=4)


def _s5_scan(proj, tables, d_skip, bsz, seq):
    bblk, cblk, pn, pp = tables
    nt = bblk.shape[0]
    w2 = 2 * S5_TILE_STATES
    tab = lambda rows: pl.BlockSpec((None, rows, w2), lambda j, b: (j, 0, 0))
    blocks = [((seq, V7X_LANES), F32), ((V7X_LANES, w2), BF16), ((V7X_LANES, w2), BF16),
              ((S5_CHUNK, w2), F32), ((S5_CHUNK, w2), F32), ((1, V7X_LANES), F32),
              ((seq, V7X_LANES), F32)]
    return pl.pallas_call(
        functools.partial(_s5_kernel, n_chunks=seq // S5_CHUNK),
        grid=(nt, bsz),
        in_specs=[pl.BlockSpec((seq, V7X_LANES), lambda j, b: (b, j)),
                  tab(V7X_LANES), tab(V7X_LANES), tab(S5_CHUNK), tab(S5_CHUNK),
                  pl.BlockSpec((1, V7X_LANES), lambda j, b: (0, j))],
        out_specs=pl.BlockSpec((seq, V7X_LANES), lambda j, b: (b, j)),
        out_shape=jax.ShapeDtypeStruct((bsz * seq, SSM_WIDTH), F32),
        compiler_params=_params(("parallel", "parallel"), blocks),
        name="s5_scan",
    )(proj, bblk, cblk, pn, pp, d_skip.reshape(1, SSM_WIDTH))


def _glu_kernel(y_ref, w_ref, b_ref, g_ref, o_ref):
    y = y_ref[...]
    z = jnp.dot(y.astype(BF16), w_ref[...], preferred_element_type=F32) + b_ref[...]
    v = y * jax.nn.sigmoid(z)
    ms = jnp.mean(v * v, axis=-1, keepdims=True)
    o_ref[...] = (v * lax.rsqrt(ms + EPS) * g_ref[...]).astype(o_ref.dtype)


def _glu_norm(y, w_glu, layer, b_glu, g_out, *, tm=512):
    t, n = y.shape
    blocks = [((tm, n), F32), ((n, n), BF16), ((1, n), F32), ((1, n), F32), ((tm, n), BF16)]
    vec = pl.BlockSpec((1, n), lambda i: (0, 0))
    return pl.pallas_call(
        _glu_kernel,
        grid=(t // tm,),
        in_specs=[pl.BlockSpec((tm, n), lambda i: (i, 0)),
                  pl.BlockSpec((None, n, n), lambda i: (layer, 0, 0)), vec, vec],
        out_specs=pl.BlockSpec((tm, n), lambda i: (i, 0)),
        out_shape=jax.ShapeDtypeStruct((t, n), BF16),
        compiler_params=_params(("parallel",), blocks),
        name="glu_norm",
    )(y, w_glu, b_glu.reshape(1, n), g_out.reshape(1, n))


def _retention_tables():
    h, c, dh = RET_HEADS, RET_CHUNK, RET_HEAD_DIM
    log_gamma = np.log1p(-np.exp2(-5.0 - np.arange(h, dtype=np.float64)))
    idx = np.arange(c, dtype=np.float64)
    rel = idx[:, None] - idx[None, :]
    mask = np.where(rel >= 0, np.exp(log_gamma[:, None, None] * np.maximum(rel, 0.0)), 0.0)
    cross = np.exp(log_gamma[:, None] * (idx + 1.0))
    state = np.exp(log_gamma[:, None] * (c - 1.0 - idx))
    chunk = np.exp(log_gamma * c)
    wide = lambda v: np.broadcast_to(v[:, :, None], (h, c, dh))
    f = lambda v: jnp.asarray(np.ascontiguousarray(v).astype(np.float32))
    return f(mask), f(wide(cross)), f(wide(state)), f(np.broadcast_to(chunk[:, None, None], (h, 1, dh)))


def _retention_kernel(q_ref, k_ref, v_ref, g_ref, cos_ref, sin_ref, mask_ref, cd_ref, sd_ref,
                      ch_ref, gn_ref, o_ref, state_ref, *, chunks_per_step):
    half = RET_HEAD_DIM // 2
    c = RET_CHUNK

    @pl.when(pl.program_id(2) == 0)
    def _():
        state_ref[...] = jnp.zeros_like(state_ref)

    mask = mask_ref[...]
    cross_decay = cd_ref[...]
    state_decay = sd_ref[...]
    chunk_decay = ch_ref[...]
    gn = gn_ref[...]
    k_scale = RET_HEAD_DIM ** -0.5

    def rotate(t, cos, sin):
        t1, t2 = t[:, :half], t[:, half:]
        return jnp.concatenate([t1 * cos - t2 * sin, t1 * sin + t2 * cos], axis=1)

    for ci in range(chunks_per_step):
        rows = pl.ds(ci * c, c)
        cos = cos_ref[rows, :]
        sin = sin_ref[rows, :]
        q = rotate(q_ref[rows, :], cos, sin)
        k = rotate(k_ref[rows, :], cos, sin) * k_scale
        v = v_ref[rows, :].astype(BF16)
        qb = q.astype(BF16)
        state = state_ref[...]
        scores = lax.dot_general(qb, k.astype(BF16), (((1,), (1,)), ((), ())),
                                 preferred_element_type=F32) * mask
        inner = jnp.dot(scores.astype(BF16), v, preferred_element_type=F32)
        cross = jnp.dot(qb, state.astype(BF16), preferred_element_type=F32) * cross_decay
        kv = lax.dot_general((k * state_decay).astype(BF16), v, (((0,), (0,)), ((), ())),
                             preferred_element_type=F32)
        state_ref[...] = state * chunk_decay + kv
        out = inner + cross
        mu = jnp.mean(out, axis=-1, keepdims=True)
        dev = out - mu
        var = jnp.mean(dev * dev, axis=-1, keepdims=True)
        g = g_ref[rows, :]
        o_ref[rows, :] = (dev * lax.rsqrt(var + EPS) * gn * (g * jax.nn.sigmoid(g))).astype(o_ref.dtype)


def _retention(proj, cos, sin, g_norm, bsz, seq, *, chunks_per_step=4):
    dh, h = RET_HEAD_DIM, RET_HEADS
    rb = chunks_per_step * RET_CHUNK
    steps = seq // rb
    mask, cd, sd, ch = _retention_tables()
    first = SSM_WIDTH // dh
    col = lambda which: pl.BlockSpec(
        (rb, dh), lambda b, hh, s, which=which: (b * steps + s, first + which * h + hh))
    rope = pl.BlockSpec((rb, dh // 2), lambda b, hh, s: (b * steps + s, 0))
    per_head = lambda rows: pl.BlockSpec((None, rows, dh), lambda b, hh, s: (hh, 0, 0))
    blocks = ([((rb, dh), F32)] * 4 + [((rb, dh // 2), F32)] * 2
              + [((RET_CHUNK, RET_CHUNK), F32), ((RET_CHUNK, dh), F32), ((RET_CHUNK, dh), F32),
                 ((1, dh), F32), ((1, dh), F32), ((rb, dh), BF16)])
    return pl.pallas_call(
        functools.partial(_retention_kernel, chunks_per_step=chunks_per_step),
        grid=(bsz, h, steps),
        in_specs=[col(0), col(1), col(2), col(3), rope, rope,
                  pl.BlockSpec((None, RET_CHUNK, RET_CHUNK), lambda b, hh, s: (hh, 0, 0)),
                  per_head(RET_CHUNK), per_head(RET_CHUNK), per_head(1),
                  pl.BlockSpec((1, dh), lambda b, hh, s: (0, hh))],
        out_specs=pl.BlockSpec((rb, dh), lambda b, hh, s: (b * steps + s, hh)),
        out_shape=jax.ShapeDtypeStruct((bsz * seq, RET_WIDTH), BF16),
        scratch_shapes=[pltpu.VMEM((dh, dh), F32)],
        compiler_params=_params(("parallel", "parallel", "arbitrary"), blocks, [((dh, dh), F32)]),
        name="retention",
    )(proj, proj, proj, proj, cos, sin, mask, cd, sd, ch, g_norm.reshape(1, RET_WIDTH))


def kernel(x, c, positions, w_ada, b_ada, g_mix, w_in, lam_re, lam_im, log_dt, b_re, b_im, c_re,
           c_im, d_skip, w_glu, b_glu, g_ssm_out, g_ret_norm, w_out, g_mlp, w_up, w_down, g_final):
    bsz, seq, d = x.shape
    depth = w_ada.shape[0]
    x2 = x.reshape(bsz * seq, d)
    mod = _ada_mod(c, w_ada, b_ada)
    cos, sin = _rope_tables(positions)
    w_in_b, w_glu_b, w_out_b, w_up_b, w_down_b = (
        w.astype(BF16) for w in (w_in, w_glu, w_out, w_up, w_down))
    for i in range(depth):
        shift1, scale1, gate1, shift2, scale2, gate2 = jnp.split(mod[i], 6, axis=-1)

        h = _normmod(x2, g_mix[i], scale1, shift1, seq)
        proj = _matmul([h], w_in_b, i, tm=1024, tn=1024, tk=d, out_dtype=F32, name="in_proj")
        tables = _s5_prep(lam_re[i], lam_im[i], log_dt[i], b_re[i], b_im[i], c_re[i], c_im[i])
        y = _s5_scan(proj, tables, d_skip[i], bsz, seq)
        ssm_out = _glu_norm(y, w_glu_b, i, b_glu[i], g_ssm_out[i])
        ret_out = _retention(proj, cos, sin, g_ret_norm[i], bsz, seq)
        x2 = _matmul([ssm_out, ret_out], w_out_b, i, tm=1024, tn=1024, tk=d, out_dtype=F32,
                     epilogue=_epi_resid, resid=x2, gate=gate1, seq=seq, name="out_proj")

        h = _normmod(x2, g_mlp[i], scale2, shift2, seq)
        a = _matmul([h], w_up_b, i, tm=1024, tn=1024, tk=d, out_dtype=BF16,
                    epilogue=_epi_relu2, name="mlp_up")
        x2 = _matmul([a], w_down_b, i, tm=1024, tn=1024, tk=2048, out_dtype=F32,
                     epilogue=_epi_resid, resid=x2, gate=gate2, seq=seq, name="mlp_down")
    return _rmsnorm(x2, g_final).reshape(bsz, seq, d)
```

```python
import functools

import numpy as np
import jax
import jax.numpy as jnp
from jax import lax
from jax.experimental import pallas as pl
from jax.experimental.pallas import tpu as pltpu

F32 = jnp.float32
BF16 = jnp.bfloat16

SSM_WIDTH = 2048
RET_WIDTH = 2048
SSM_GROUP = 16
SSM_STATE = 64
RET_HEAD_DIM = 256
RET_HEADS = 8
RET_CHUNK = 128
ROPE_BASE = 10000.0
EPS = 1e-6

V7X_LANES = 128
V7X_VMEM_BYTES = 64 * 1024 * 1024
V7X_COMPILER_SCRATCH_BYTES = 6 * 1024 * 1024

S5_CHUNK = 256
S5_GROUPS_PER_TILE = V7X_LANES // SSM_GROUP
S5_TILE_STATES = S5_GROUPS_PER_TILE * SSM_STATE


def _nbytes(shape, dtype):
    return int(np.prod(shape)) * jnp.dtype(dtype).itemsize


def _params(semantics, pipelined_blocks, scratch_blocks=()):
    need = 2 * sum(_nbytes(s, d) for s, d in pipelined_blocks)
    need += sum(_nbytes(s, d) for s, d in scratch_blocks)
    limit = min(need + V7X_COMPILER_SCRATCH_BYTES, V7X_VMEM_BYTES - 4 * 1024 * 1024)
    return pltpu.CompilerParams(dimension_semantics=semantics, vmem_limit_bytes=limit)


def _ada_kernel(c_ref, w_ref, b_ref, o_ref):
    c = c_ref[...]
    c_act = c * jax.nn.sigmoid(c)
    o_ref[...] = jnp.dot(c_act.astype(BF16), w_ref[...].astype(BF16),
                         preferred_element_type=F32) + b_ref[...]


def _ada_mod(c, w_ada, b_ada, *, tn=1024):
    depth, d, n = w_ada.shape
    bsz = c.shape[0]
    rows = 8
    c_pad = jnp.zeros((rows, d), F32).at[:bsz].set(c)
    blocks = [((rows, d), F32), ((d, tn), F32), ((1, tn), F32), ((rows, tn), F32)]
    out = pl.pallas_call(
        _ada_kernel,
        grid=(depth, n // tn),
        in_specs=[pl.BlockSpec((rows, d), lambda l, j: (0, 0)),
                  pl.BlockSpec((None, d, tn), lambda l, j: (l, 0, j)),
                  pl.BlockSpec((None, 1, tn), lambda l, j: (l, 0, j))],
        out_specs=pl.BlockSpec((None, rows, tn), lambda l, j: (l, 0, j)),
        out_shape=jax.ShapeDtypeStruct((depth, rows, n), F32),
        compiler_params=_params(("parallel", "parallel"), blocks),
        name="ada_mod",
    )(c_pad, w_ada, b_ada.reshape(depth, 1, n))
    return out[:, :bsz]


def _normmod_kernel(x_ref, g_ref, sc_ref, sh_ref, o_ref):
    x = x_ref[...]
    ms = jnp.mean(x * x, axis=-1, keepdims=True)
    y = x * lax.rsqrt(ms + EPS) * g_ref[...]
    o_ref[...] = (y * (1.0 + sc_ref[...]) + sh_ref[...]).astype(o_ref.dtype)


def _normmod(x2, g, scale, shift, seq, *, tm=512):
    t, d = x2.shape
    bsz = scale.shape[0]
    blocks = [((tm, d), F32), ((1, d), F32), ((1, d), F32), ((1, d), F32), ((tm, d), BF16)]
    per_b = pl.BlockSpec((None, 1, d), lambda i: (i * tm // seq, 0, 0))
    return pl.pallas_call(
        _normmod_kernel,
        grid=(t // tm,),
        in_specs=[pl.BlockSpec((tm, d), lambda i: (i, 0)),
                  pl.BlockSpec((1, d), lambda i: (0, 0)), per_b, per_b],
        out_specs=pl.BlockSpec((tm, d), lambda i: (i, 0)),
        out_shape=jax.ShapeDtypeStruct((t, d), BF16),
        compiler_params=_params(("parallel",), blocks, [((tm, d), F32)]),
        name="normmod",
    )(x2, g.reshape(1, d), scale.reshape(bsz, 1, d), shift.reshape(bsz, 1, d))


def _rmsnorm_kernel(x_ref, g_ref, o_ref):
    x = x_ref[...]
    ms = jnp.mean(x * x, axis=-1, keepdims=True)
    o_ref[...] = x * lax.rsqrt(ms + EPS) * g_ref[...]


def _rmsnorm(x2, g, *, tm=512):
    t, d = x2.shape
    blocks = [((tm, d), F32), ((1, d), F32), ((tm, d), F32)]
    return pl.pallas_call(
        _rmsnorm_kernel,
        grid=(t // tm,),
        in_specs=[pl.BlockSpec((tm, d), lambda i: (i, 0)),
                  pl.BlockSpec((1, d), lambda i: (0, 0))],
        out_specs=pl.BlockSpec((tm, d), lambda i: (i, 0)),
        out_shape=jax.ShapeDtypeStruct((t, d), F32),
        compiler_params=_params(("parallel",), blocks, [((tm, d), F32)]),
        name="final_rmsnorm",
    )(x2, g.reshape(1, d))


def _epi_none(acc):
    return acc


def _epi_relu2(acc):
    r = jnp.maximum(acc, 0.0)
    return r * r


def _epi_resid(acc, x_ref, gate_ref):
    return x_ref[...] + gate_ref[...] * acc


def _mm_kernel(*refs, n_a, nk, epilogue):
    a_refs, w_ref, rest = refs[:n_a], refs[n_a], refs[n_a + 1:]

    def product():
        total, row = None, 0
        for a_ref in a_refs:
            kk = a_ref.shape[1]
            part = jnp.dot(a_ref[...], w_ref[row:row + kk, :], preferred_element_type=F32)
            total = part if total is None else total + part
            row += kk
        return total

    if nk == 1:
        *extra, o_ref = rest
        o_ref[...] = epilogue(product(), *extra).astype(o_ref.dtype)
        return
    *extra, o_ref, acc_ref = rest
    k = pl.program_id(2)

    @pl.when(k == 0)
    def _():
        acc_ref[...] = jnp.zeros_like(acc_ref)

    acc_ref[...] += product()

    @pl.when(k == nk - 1)
    def _():
        o_ref[...] = epilogue(acc_ref[...], *extra).astype(o_ref.dtype)


def _matmul(a_list, w, layer, *, tm, tn, tk, out_dtype, epilogue=_epi_none, resid=None,
            gate=None, seq=None, name):
    m = a_list[0].shape[0]
    _, kdim, n = w.shape
    nk = kdim // tk
    if len(a_list) == 1:
        in_specs = [pl.BlockSpec((tm, tk), lambda i, j, k: (i, k))]
        blocks = [((tm, tk), a_list[0].dtype)]
    else:
        assert nk == 1 and sum(a.shape[1] for a in a_list) == kdim
        in_specs = [pl.BlockSpec((tm, a.shape[1]), lambda i, j, k: (i, 0)) for a in a_list]
        blocks = [((tm, a.shape[1]), a.dtype) for a in a_list]
    in_specs.append(pl.BlockSpec((None, tk, tn), lambda i, j, k: (layer, k, j)))
    blocks += [((tk, tn), w.dtype), ((tm, tn), out_dtype)]
    args = list(a_list) + [w]
    if resid is not None:
        bsz = gate.shape[0]
        in_specs += [pl.BlockSpec((tm, tn), lambda i, j, k: (i, j)),
                     pl.BlockSpec((None, 1, tn), lambda i, j, k: (i * tm // seq, 0, j))]
        blocks += [((tm, tn), F32), ((1, tn), F32)]
        args += [resid, gate.reshape(bsz, 1, n)]
    scratch = [] if nk == 1 else [pltpu.VMEM((tm, tn), F32)]
    scratch_blocks = [((tm, tn), F32)] * (1 if nk == 1 else 2)
    return pl.pallas_call(
        functools.partial(_mm_kernel, n_a=len(a_list), nk=nk, epilogue=epilogue),
        grid=(m // tm, n // tn, nk),
        in_specs=in_specs,
        out_specs=pl.BlockSpec((tm, tn), lambda i, j, k: (i, j)),
        out_shape=jax.ShapeDtypeStruct((m, n), out_dtype),
        scratch_shapes=scratch,
        compiler_params=_params(("parallel", "parallel", "arbitrary"), blocks, scratch_blocks),
        name=name,
    )(*args)


def _mm_cast_kernel(a_ref, wc_ref, o_ref, wb_ref, *, epilogue):
    j = pl.program_id(0)
    i = pl.program_id(1)
    nj = pl.num_programs(0) - 1
    rows = wc_ref.shape[0]

    @pl.when(j < nj)
    def _():
        wb_ref[j % 2, pl.ds(pl.multiple_of(i * rows, rows), rows), :] = wc_ref[...].astype(BF16)

    @pl.when(j > 0)
    def _():
        prod = jnp.dot(a_ref[...], wb_ref[(j - 1) % 2], preferred_element_type=F32)
        o_ref[...] = epilogue(prod).astype(o_ref.dtype)


def _matmul_cast(a, w, layer, *, tm, tn, out_dtype, epilogue=_epi_none, name):
    m, kdim = a.shape
    n = w.shape[2]
    ni, nj = m // tm, n // tn
    rows = kdim // ni
    row_blk = lambda j, i: jnp.where(j > 0, i, 0)
    col_blk = lambda j: jnp.maximum(j - 1, 0)
    blocks = [((tm, kdim), a.dtype), ((rows, tn), w.dtype), ((tm, tn), out_dtype)]
    scratch_blocks = [((2, kdim, tn), BF16), ((tm, tn), F32)]
    return pl.pallas_call(
        functools.partial(_mm_cast_kernel, epilogue=epilogue),
        grid=(nj + 1, ni),
        in_specs=[pl.BlockSpec((tm, kdim), lambda j, i: (row_blk(j, i), 0)),
                  pl.BlockSpec((None, rows, tn),
                               lambda j, i: (layer, jnp.where(j < nj, i, ni - 1),
                                             jnp.minimum(j, nj - 1)))],
        out_specs=pl.BlockSpec((tm, tn), lambda j, i: (row_blk(j, i), col_blk(j))),
        out_shape=jax.ShapeDtypeStruct((m, n), out_dtype),
        scratch_shapes=[pltpu.VMEM((2, kdim, tn), BF16)],
        compiler_params=_params(("arbitrary", "arbitrary"), blocks, scratch_blocks),
        name=name,
    )(a, w)


def _rope_kernel(pos_ref, inv_ref, cos_ref, sin_ref):
    ang = pos_ref[...].astype(F32) * inv_ref[...]
    cos_ref[...] = jnp.cos(ang)
    sin_ref[...] = jnp.sin(ang)


def _rope_tables(positions, *, tm=1024):
    t = positions.size
    half = RET_HEAD_DIM // 2
    inv = ROPE_BASE ** (-np.arange(0, RET_HEAD_DIM, 2, dtype=np.float64) / RET_HEAD_DIM)
    inv = jnp.asarray(inv.astype(np.float32)).reshape(1, half)
    blocks = [((tm, 1), jnp.int32), ((1, half), F32), ((tm, half), F32), ((tm, half), F32)]
    return pl.pallas_call(
        _rope_kernel,
        grid=(t // tm,),
        in_specs=[pl.BlockSpec((tm, 1), lambda i: (i, 0)),
                  pl.BlockSpec((1, half), lambda i: (0, 0))],
        out_specs=[pl.BlockSpec((tm, half), lambda i: (i, 0))] * 2,
        out_shape=[jax.ShapeDtypeStruct((t, half), F32)] * 2,
        compiler_params=_params(("parallel",), blocks),
        name="rope_tables",
    )(positions.reshape(t, 1), inv)


def _s5_prep_kernel(lr_ref, li_ref, ldt_ref, br_ref, bi_ref, cr_ref, ci_ref,
                    bblk_ref, cblk_ref, pn_ref, pp_ref):
    w = S5_TILE_STATES
    lr = lr_ref[...]
    li = li_ref[...]
    dt = jnp.exp(ldt_ref[...])
    a = lr * dt
    th = li * dt
    mag = jnp.exp(a)
    lbr = mag * jnp.cos(th)
    lbi = mag * jnp.sin(th)
    den = lr * lr + li * li
    nr = lbr - 1.0
    coef_r = (nr * lr + lbi * li) / den
    coef_i = (lbi * lr - nr * li) / den
    br = br_ref[...]
    bi = bi_ref[...]
    bbar_r = coef_r * br - coef_i * bi
    bbar_i = coef_r * bi + coef_i * br

    rows = lax.broadcasted_iota(jnp.int32, (V7X_LANES, w), 0)
    cols = lax.broadcasted_iota(jnp.int32, (V7X_LANES, w), 1)
    same_group = ((rows >> (SSM_GROUP.bit_length() - 1))
                  == (cols >> (SSM_STATE.bit_length() - 1)))
    reps = (S5_GROUPS_PER_TILE, 1)

    def blockdiag(v):
        return jnp.where(same_group, jnp.tile(v, reps), 0.0)

    bblk_ref[:, :w] = blockdiag(bbar_r).astype(bblk_ref.dtype)
    bblk_ref[:, w:] = blockdiag(bbar_i).astype(bblk_ref.dtype)
    cblk_ref[:, :w] = blockdiag(cr_ref[...]).astype(cblk_ref.dtype)
    cblk_ref[:, w:] = blockdiag(-ci_ref[...]).astype(cblk_ref.dtype)

    m = (lax.broadcasted_iota(jnp.int32, (S5_CHUNK, w), 0) + 1).astype(F32)
    cs = jnp.cos(m * th)
    sn = jnp.sin(m * th)
    grow = jnp.exp(m * a)
    shrink = jnp.exp(-(m * a))
    pp_ref[:, :w] = grow * cs
    pp_ref[:, w:] = grow * sn
    pn_ref[:, :w] = shrink * cs
    pn_ref[:, w:] = -(shrink * sn)


def _s5_prep(lam_re, lam_im, log_dt, b_re, b_im, c_re, c_im):
    g, p, h = b_re.shape
    gl, w = S5_GROUPS_PER_TILE, S5_TILE_STATES
    nt = g // gl
    row = lambda v: v.reshape(nt, 1, w)
    bt = lambda v: v.reshape(nt, gl, p, h).transpose(0, 3, 1, 2).reshape(nt, h, w)
    ct = lambda v: v.reshape(nt, gl, h, p).transpose(0, 2, 1, 3).reshape(nt, h, w)
    row_spec = pl.BlockSpec((None, 1, w), lambda j: (j, 0, 0))
    mat_spec = pl.BlockSpec((None, h, w), lambda j: (j, 0, 0))
    blk_spec = pl.BlockSpec((None, V7X_LANES, 2 * w), lambda j: (j, 0, 0))
    pow_spec = pl.BlockSpec((None, S5_CHUNK, 2 * w), lambda j: (j, 0, 0))
    blocks = ([((1, w), F32)] * 3 + [((h, w), F32)] * 4 + [((V7X_LANES, 2 * w), BF16)] * 2
              + [((S5_CHUNK, 2 * w), F32)] * 2)
    return pl.pallas_call(
        _s5_prep_kernel,
        grid=(nt,),
        in_specs=[row_spec] * 3 + [mat_spec] * 4,
        out_specs=[blk_spec, blk_spec, pow_spec, pow_spec],
        out_shape=[jax.ShapeDtypeStruct((nt, V7X_LANES, 2 * w), BF16)] * 2
        + [jax.ShapeDtypeStruct((nt, S5_CHUNK, 2 * w), F32)] * 2,
        compiler_params=_params(("parallel",), blocks),
        name="s5_prep",
    )(row(lam_re), row(lam_im), row(jnp.repeat(log_dt, p)), bt(b_re), bt(b_im), ct(c_re), ct(c_im))


def _s5_kernel(u_ref, bblk_ref, cblk_ref, pn_ref, pp_ref, d_ref, y_ref, *, n_chunks):
    w = S5_TILE_STATES
    lc = S5_CHUNK
    bblk = bblk_ref[...]
    cblk = cblk_ref[...]
    d = d_ref[...]
    r = lax.broadcasted_iota(jnp.int32, (lc, lc), 0)
    c = lax.broadcasted_iota(jnp.int32, (lc, lc), 1)
    tri = (r >= c).astype(BF16)

    def chunk(t, carry):
        cr, ci = carry
        rows = pl.ds(pl.multiple_of(t * lc, lc), lc)
        u = u_ref[rows, :]
        bu = jnp.dot(u.astype(BF16), bblk, preferred_element_type=F32)
        br, bi = bu[:, :w], bu[:, w:]
        pnr, pni = pn_ref[:, :w], pn_ref[:, w:]
        z = jnp.concatenate([pnr * br - pni * bi, pnr * bi + pni * br], axis=1)
        ws = jnp.dot(tri, z.astype(BF16), preferred_element_type=F32)
        wr = ws[:, :w] + cr
        wi = ws[:, w:] + ci
        ppr, ppi = pp_ref[:, :w], pp_ref[:, w:]
        xr = ppr * wr - ppi * wi
        xi = ppr * wi + ppi * wr
        x = jnp.concatenate([xr, xi], axis=1).astype(BF16)
        y = lax.dot_general(x, cblk, (((1,), (1,)), ((), ())), preferred_element_type=F32)
        y_ref[rows, :] = jax.nn.gelu(y + d * u)
        return xr[lc - 1:, :], xi[lc - 1:, :]

    zero = jnp.zeros((1, w), F32)
    lax.fori_loop(0, n_chunks, chunk, (zero, zero), unroll=4)


def _s5_scan(proj, tables, d_skip, bsz, seq):
    bblk, cblk, pn, pp = tables
    nt = bblk.shape[0]
    w2 = 2 * S5_TILE_STATES
    tab = lambda rows: pl.BlockSpec((None, rows, w2), lambda j, b: (j, 0, 0))
    blocks = [((seq, V7X_LANES), F32), ((V7X_LANES, w2), BF16), ((V7X_LANES, w2), BF16),
              ((S5_CHUNK, w2), F32), ((S5_CHUNK, w2), F32), ((1, V7X_LANES), F32),
              ((seq, V7X_LANES), F32)]
    return pl.pallas_call(
        functools.partial(_s5_kernel, n_chunks=seq // S5_CHUNK),
        grid=(nt, bsz),
        in_specs=[pl.BlockSpec((seq, V7X_LANES), lambda j, b: (b, j)),
                  tab(V7X_LANES), tab(V7X_LANES), tab(S5_CHUNK), tab(S5_CHUNK),
                  pl.BlockSpec((1, V7X_LANES), lambda j, b: (0, j))],
        out_specs=pl.BlockSpec((seq, V7X_LANES), lambda j, b: (b, j)),
        out_shape=jax.ShapeDtypeStruct((bsz * seq, SSM_WIDTH), F32),
        compiler_params=_params(("parallel", "parallel"), blocks),
        name="s5_scan",
    )(proj, bblk, cblk, pn, pp, d_skip.reshape(1, SSM_WIDTH))


def _glu_kernel(y_ref, w_ref, b_ref, g_ref, o_ref):
    y = y_ref[...]
    z = jnp.dot(y.astype(BF16), w_ref[...], preferred_element_type=F32) + b_ref[...]
    v = y * jax.nn.sigmoid(z)
    ms = jnp.mean(v * v, axis=-1, keepdims=True)
    o_ref[...] = (v * lax.rsqrt(ms + EPS) * g_ref[...]).astype(o_ref.dtype)


def _glu_norm(y, w_glu, layer, b_glu, g_out, *, tm=512):
    t, n = y.shape
    blocks = [((tm, n), F32), ((n, n), BF16), ((1, n), F32), ((1, n), F32), ((tm, n), BF16)]
    vec = pl.BlockSpec((1, n), lambda i: (0, 0))
    return pl.pallas_call(
        _glu_kernel,
        grid=(t // tm,),
        in_specs=[pl.BlockSpec((tm, n), lambda i: (i, 0)),
                  pl.BlockSpec((None, n, n), lambda i: (layer, 0, 0)), vec, vec],
        out_specs=pl.BlockSpec((tm, n), lambda i: (i, 0)),
        out_shape=jax.ShapeDtypeStruct((t, n), BF16),
        compiler_params=_params(("parallel",), blocks),
        name="glu_norm",
    )(y, w_glu, b_glu.reshape(1, n), g_out.reshape(1, n))


def _retention_tables():
    h, c, dh = RET_HEADS, RET_CHUNK, RET_HEAD_DIM
    log_gamma = np.log1p(-np.exp2(-5.0 - np.arange(h, dtype=np.float64)))
    idx = np.arange(c, dtype=np.float64)
    rel = idx[:, None] - idx[None, :]
    mask = np.where(rel >= 0, np.exp(log_gamma[:, None, None] * np.maximum(rel, 0.0)), 0.0)
    cross = np.exp(log_gamma[:, None] * (idx + 1.0))
    state = np.exp(log_gamma[:, None] * (c - 1.0 - idx))
    chunk = np.exp(log_gamma * c)
    wide = lambda v: np.broadcast_to(v[:, :, None], (h, c, dh))
    f = lambda v: jnp.asarray(np.ascontiguousarray(v).astype(np.float32))
    return f(mask), f(wide(cross)), f(wide(state)), f(np.broadcast_to(chunk[:, None, None], (h, 1, dh)))


def _retention_kernel(q_ref, k_ref, v_ref, g_ref, cos_ref, sin_ref, mask_ref, cd_ref, sd_ref,
                      ch_ref, gn_ref, o_ref, state_ref, *, chunks_per_step):
    half = RET_HEAD_DIM // 2
    c = RET_CHUNK

    @pl.when(pl.program_id(2) == 0)
    def _():
        state_ref[...] = jnp.zeros_like(state_ref)

    mask = mask_ref[...]
    cross_decay = cd_ref[...]
    state_decay = sd_ref[...]
    chunk_decay = ch_ref[...]
    gn = gn_ref[...]
    k_scale = RET_HEAD_DIM ** -0.5

    def rotate(t, cos, sin):
        t1, t2 = t[:, :half], t[:, half:]
        return jnp.concatenate([t1 * cos - t2 * sin, t1 * sin + t2 * cos], axis=1)

    for ci in range(chunks_per_step):
        rows = pl.ds(ci * c, c)
        cos = cos_ref[rows, :]
        sin = sin_ref[rows, :]
        q = rotate(q_ref[rows, :], cos, sin)
        k = rotate(k_ref[rows, :], cos, sin) * k_scale
        v = v_ref[rows, :].astype(BF16)
        qb = q.astype(BF16)
        state = state_ref[...]
        scores = lax.dot_general(qb, k.astype(BF16), (((1,), (1,)), ((), ())),
                                 preferred_element_type=F32) * mask
        inner = jnp.dot(scores.astype(BF16), v, preferred_element_type=F32)
        cross = jnp.dot(qb, state.astype(BF16), preferred_element_type=F32) * cross_decay
        kv = lax.dot_general((k * state_decay).astype(BF16), v, (((0,), (0,)), ((), ())),
                             preferred_element_type=F32)
        state_ref[...] = state * chunk_decay + kv
        out = inner + cross
        mu = jnp.mean(out, axis=-1, keepdims=True)
        dev = out - mu
        var = jnp.mean(dev * dev, axis=-1, keepdims=True)
        g = g_ref[rows, :]
        o_ref[rows, :] = (dev * lax.rsqrt(var + EPS) * gn * (g * jax.nn.sigmoid(g))).astype(o_ref.dtype)


def _retention(proj, cos, sin, g_norm, bsz, seq, *, chunks_per_step=8):
    dh, h = RET_HEAD_DIM, RET_HEADS
    rb = chunks_per_step * RET_CHUNK
    steps = seq // rb
    mask, cd, sd, ch = _retention_tables()
    first = SSM_WIDTH // dh
    col = lambda which: pl.BlockSpec(
        (rb, dh), lambda b, hh, s, which=which: (b * steps + s, first + which * h + hh))
    rope = pl.BlockSpec((rb, dh // 2), lambda b, hh, s: (b * steps + s, 0))
    per_head = lambda rows: pl.BlockSpec((None, rows, dh), lambda b, hh, s: (hh, 0, 0))
    blocks = ([((rb, dh), F32)] * 4 + [((rb, dh // 2), F32)] * 2
              + [((RET_CHUNK, RET_CHUNK), F32), ((RET_CHUNK, dh), F32), ((RET_CHUNK, dh), F32),
                 ((1, dh), F32), ((1, dh), F32), ((rb, dh), BF16)])
    return pl.pallas_call(
        functools.partial(_retention_kernel, chunks_per_step=chunks_per_step),
        grid=(bsz, h, steps),
        in_specs=[col(0), col(1), col(2), col(3), rope, rope,
                  pl.BlockSpec((None, RET_CHUNK, RET_CHUNK), lambda b, hh, s: (hh, 0, 0)),
                  per_head(RET_CHUNK), per_head(RET_CHUNK), per_head(1),
                  pl.BlockSpec((1, dh), lambda b, hh, s: (0, hh))],
        out_specs=pl.BlockSpec((rb, dh), lambda b, hh, s: (b * steps + s, hh)),
        out_shape=jax.ShapeDtypeStruct((bsz * seq, RET_WIDTH), BF16),
        scratch_shapes=[pltpu.VMEM((dh, dh), F32)],
        compiler_params=_params(("parallel", "parallel", "arbitrary"), blocks, [((dh, dh), F32)]),
        name="retention",
    )(proj, proj, proj, proj, cos, sin, mask, cd, sd, ch, g_norm.reshape(1, RET_WIDTH))


def kernel(x, c, positions, w_ada, b_ada, g_mix, w_in, lam_re, lam_im, log_dt, b_re, b_im, c_re,
           c_im, d_skip, w_glu, b_glu, g_ssm_out, g_ret_norm, w_out, g_mlp, w_up, w_down, g_final):
    bsz, seq, d = x.shape
    depth = w_ada.shape[0]
    x2 = x.reshape(bsz * seq, d)
    mod = _ada_mod(c, w_ada, b_ada)
    cos, sin = _rope_tables(positions)
    w_glu_b, w_out_b, w_down_b = (w.astype(BF16) for w in (w_glu, w_out, w_down))
    for i in range(depth):
        shift1, scale1, gate1, shift2, scale2, gate2 = jnp.split(mod[i], 6, axis=-1)

        h = _normmod(x2, g_mix[i], scale1, shift1, seq)
        proj = _matmul_cast(h, w_in, i, tm=1024, tn=1024, out_dtype=F32, name="in_proj")
        tables = _s5_prep(lam_re[i], lam_im[i], log_dt[i], b_re[i], b_im[i], c_re[i], c_im[i])
        y = _s5_scan(proj, tables, d_skip[i], bsz, seq)
        ssm_out = _glu_norm(y, w_glu_b, i, b_glu[i], g_ssm_out[i])
        ret_out = _retention(proj, cos, sin, g_ret_norm[i], bsz, seq)
        x2 = _matmul([ssm_out, ret_out], w_out_b, i, tm=1024, tn=1024, tk=d, out_dtype=F32,
                     epilogue=_epi_resid, resid=x2, gate=gate1, seq=seq, name="out_proj")

        h = _normmod(x2, g_mlp[i], scale2, shift2, seq)
        a = _matmul_cast(h, w_up, i, tm=1024, tn=1024, out_dtype=BF16, epilogue=_epi_relu2,
                         name="mlp_up")
        x2 = _matmul([a], w_down_b, i, tm=1024, tn=1024, tk=2048, out_dtype=F32,
                     epilogue=_epi_resid, resid=x2, gate=gate2, seq=seq, name="mlp_down")
    return _rmsnorm(x2, g_final).reshape(bsz, seq, d)
```

```python
import functools

import numpy as np
import jax
import jax.numpy as jnp
from jax import lax
from jax.experimental import pallas as pl
from jax.experimental.pallas import tpu as pltpu

F32 = jnp.float32
BF16 = jnp.bfloat16

SSM_WIDTH = 2048
RET_WIDTH = 2048
SSM_GROUP = 16
SSM_STATE = 64
RET_HEAD_DIM = 256
RET_HEADS = 8
RET_CHUNK = 128
ROPE_BASE = 10000.0
EPS = 1e-6

V7X_LANES = 128
V7X_VMEM_BYTES = 64 * 1024 * 1024
V7X_COMPILER_SCRATCH_BYTES = 6 * 1024 * 1024

S5_CHUNK = 256
S5_GROUPS_PER_TILE = V7X_LANES // SSM_GROUP
S5_TILE_STATES = S5_GROUPS_PER_TILE * SSM_STATE


def _nbytes(shape, dtype):
    return int(np.prod(shape)) * jnp.dtype(dtype).itemsize


def _params(semantics, pipelined_blocks, scratch_blocks=()):
    need = 2 * sum(_nbytes(s, d) for s, d in pipelined_blocks)
    need += sum(_nbytes(s, d) for s, d in scratch_blocks)
    limit = min(need + V7X_COMPILER_SCRATCH_BYTES, V7X_VMEM_BYTES - 4 * 1024 * 1024)
    return pltpu.CompilerParams(dimension_semantics=semantics, vmem_limit_bytes=limit)


def _ada_kernel(c_ref, w_ref, b_ref, o_ref):
    c = c_ref[...]
    c_act = c * jax.nn.sigmoid(c)
    o_ref[...] = jnp.dot(c_act.astype(BF16), w_ref[...].astype(BF16),
                         preferred_element_type=F32) + b_ref[...]


ADA_ROWS = 8


def _ada_mod(c_pad, w_ada, b_ada3, layer, *, tn=1024):
    _, d, n = w_ada.shape
    rows = ADA_ROWS
    blocks = [((rows, d), F32), ((d, tn), F32), ((1, tn), F32), ((rows, tn), F32)]
    return pl.pallas_call(
        _ada_kernel,
        grid=(n // tn,),
        in_specs=[pl.BlockSpec((rows, d), lambda j: (0, 0)),
                  pl.BlockSpec((None, d, tn), lambda j: (layer, 0, j)),
                  pl.BlockSpec((None, 1, tn), lambda j: (layer, 0, j))],
        out_specs=pl.BlockSpec((rows, tn), lambda j: (0, j)),
        out_shape=jax.ShapeDtypeStruct((rows, n), F32),
        compiler_params=_params(("parallel",), blocks),
        name="ada_mod",
    )(c_pad, w_ada, b_ada3)


def _normmod_kernel(x_ref, g_ref, sc_ref, sh_ref, o_ref):
    x = x_ref[...]
    ms = jnp.mean(x * x, axis=-1, keepdims=True)
    y = x * lax.rsqrt(ms + EPS) * g_ref[...]
    o_ref[...] = (y * (1.0 + sc_ref[...]) + sh_ref[...]).astype(o_ref.dtype)


def _normmod(x2, g, scale, shift, seq, *, tm=512):
    t, d = x2.shape
    bsz = scale.shape[0]
    blocks = [((tm, d), F32), ((1, d), F32), ((1, d), F32), ((1, d), F32), ((tm, d), BF16)]
    per_b = pl.BlockSpec((None, 1, d), lambda i: (i * tm // seq, 0, 0))
    return pl.pallas_call(
        _normmod_kernel,
        grid=(t // tm,),
        in_specs=[pl.BlockSpec((tm, d), lambda i: (i, 0)),
                  pl.BlockSpec((1, d), lambda i: (0, 0)), per_b, per_b],
        out_specs=pl.BlockSpec((tm, d), lambda i: (i, 0)),
        out_shape=jax.ShapeDtypeStruct((t, d), BF16),
        compiler_params=_params(("parallel",), blocks, [((tm, d), F32)]),
        name="normmod",
    )(x2, g.reshape(1, d), scale.reshape(bsz, 1, d), shift.reshape(bsz, 1, d))


def _rmsnorm_kernel(x_ref, g_ref, o_ref):
    x = x_ref[...]
    ms = jnp.mean(x * x, axis=-1, keepdims=True)
    o_ref[...] = x * lax.rsqrt(ms + EPS) * g_ref[...]


def _rmsnorm(x2, g, *, tm=512):
    t, d = x2.shape
    blocks = [((tm, d), F32), ((1, d), F32), ((tm, d), F32)]
    return pl.pallas_call(
        _rmsnorm_kernel,
        grid=(t // tm,),
        in_specs=[pl.BlockSpec((tm, d), lambda i: (i, 0)),
                  pl.BlockSpec((1, d), lambda i: (0, 0))],
        out_specs=pl.BlockSpec((tm, d), lambda i: (i, 0)),
        out_shape=jax.ShapeDtypeStruct((t, d), F32),
        compiler_params=_params(("parallel",), blocks, [((tm, d), F32)]),
        name="final_rmsnorm",
    )(x2, g.reshape(1, d))


def _epi_none(acc):
    return acc


def _epi_relu2(acc):
    r = jnp.maximum(acc, 0.0)
    return r * r


def _epi_resid(acc, x_ref, gate_ref):
    return x_ref[...] + gate_ref[...] * acc


def _mm_kernel(*refs, n_a, nk, epilogue, has_ada):
    a_refs, w_ref, rest = refs[:n_a], refs[n_a], refs[n_a + 1:]
    if has_ada:
        n_main = len(rest) - (5 if nk == 1 else 6)
        c_ref, wa_ref, ba_ref = rest[n_main:n_main + 3]
        mod_ref = rest[n_main + 4]
        _ada_kernel(c_ref, wa_ref, ba_ref, mod_ref)
        rest = rest[:n_main] + (rest[n_main + 3],) + rest[n_main + 5:]

    def product():
        total, row = None, 0
        for a_ref in a_refs:
            kk = a_ref.shape[1]
            part = jnp.dot(a_ref[...], w_ref[row:row + kk, :], preferred_element_type=F32)
            total = part if total is None else total + part
            row += kk
        return total

    if nk == 1:
        *extra, o_ref = rest
        o_ref[...] = epilogue(product(), *extra).astype(o_ref.dtype)
        return
    *extra, o_ref, acc_ref = rest
    k = pl.program_id(2)

    @pl.when(k == 0)
    def _():
        acc_ref[...] = jnp.zeros_like(acc_ref)

    acc_ref[...] += product()

    @pl.when(k == nk - 1)
    def _():
        o_ref[...] = epilogue(acc_ref[...], *extra).astype(o_ref.dtype)


def _matmul(a_list, w, layer, *, tm, tn, tk, out_dtype, epilogue=_epi_none, resid=None,
            gate=None, seq=None, ada=None, name):
    m = a_list[0].shape[0]
    _, kdim, n = w.shape
    nk = kdim // tk
    if len(a_list) == 1:
        in_specs = [pl.BlockSpec((tm, tk), lambda i, j, k: (i, k))]
        blocks = [((tm, tk), a_list[0].dtype)]
    else:
        assert nk == 1 and sum(a.shape[1] for a in a_list) == kdim
        in_specs = [pl.BlockSpec((tm, a.shape[1]), lambda i, j, k: (i, 0)) for a in a_list]
        blocks = [((tm, a.shape[1]), a.dtype) for a in a_list]
    in_specs.append(pl.BlockSpec((None, tk, tn), lambda i, j, k: (layer, k, j)))
    blocks += [((tk, tn), w.dtype), ((tm, tn), out_dtype)]
    args = list(a_list) + [w]
    if resid is not None:
        bsz = gate.shape[0]
        in_specs += [pl.BlockSpec((tm, tn), lambda i, j, k: (i, j)),
                     pl.BlockSpec((None, 1, tn), lambda i, j, k: (i * tm // seq, 0, j))]
        blocks += [((tm, tn), F32), ((1, tn), F32)]
        args += [resid, gate.reshape(bsz, 1, n)]
    scratch = [] if nk == 1 else [pltpu.VMEM((tm, tn), F32)]
    scratch_blocks = [((tm, tn), F32)] * (1 if nk == 1 else 2)
    out_specs = pl.BlockSpec((tm, tn), lambda i, j, k: (i, j))
    out_shape = jax.ShapeDtypeStruct((m, n), out_dtype)
    if ada is not None:
        c_pad, w_ada, b_ada3, ada_layer = ada
        _, d_ada, n_ada = w_ada.shape
        n_blk = n_ada // V7X_LANES
        nj = n // tn
        assert n_blk <= (m // tm) * nj * nk
        blk = lambda i, j, k: jnp.minimum((i * nj + j) * nk + k, n_blk - 1)
        in_specs += [pl.BlockSpec((ADA_ROWS, d_ada), lambda i, j, k: (0, 0)),
                     pl.BlockSpec((None, d_ada, V7X_LANES), lambda i, j, k: (ada_layer, 0, blk(i, j, k))),
                     pl.BlockSpec((None, 1, V7X_LANES), lambda i, j, k: (ada_layer, 0, blk(i, j, k)))]
        out_specs = [out_specs, pl.BlockSpec((ADA_ROWS, V7X_LANES), lambda i, j, k: (0, blk(i, j, k)))]
        out_shape = [out_shape, jax.ShapeDtypeStruct((ADA_ROWS, n_ada), F32)]
        blocks += [((ADA_ROWS, d_ada), F32), ((d_ada, V7X_LANES), F32), ((1, V7X_LANES), F32),
                   ((ADA_ROWS, V7X_LANES), F32)]
        args += [c_pad, w_ada, b_ada3]
    return pl.pallas_call(
        functools.partial(_mm_kernel, n_a=len(a_list), nk=nk, epilogue=epilogue,
                          has_ada=ada is not None),
        grid=(m // tm, n // tn, nk),
        in_specs=in_specs,
        out_specs=out_specs,
        out_shape=out_shape,
        scratch_shapes=scratch,
        compiler_params=_params(("arbitrary",) * 3 if ada is not None
                                else ("parallel", "parallel", "arbitrary"), blocks, scratch_blocks),
        name=name,
    )(*args)


def _mm_cast_kernel(a_ref, wc_ref, *rest, epilogue, has_side):
    if has_side:
        side_ref, o_ref, side_o_ref, wb_ref = rest
        side_o_ref[...] = side_ref[...].astype(BF16)
    else:
        o_ref, wb_ref = rest
    j = pl.program_id(0)
    i = pl.program_id(1)
    nj = pl.num_programs(0) - 1
    rows = wc_ref.shape[0]

    @pl.when(j < nj)
    def _():
        wb_ref[j % 2, pl.ds(pl.multiple_of(i * rows, rows), rows), :] = wc_ref[...].astype(BF16)

    @pl.when(j > 0)
    def _():
        prod = jnp.dot(a_ref[...], wb_ref[(j - 1) % 2], preferred_element_type=F32)
        o_ref[...] = epilogue(prod).astype(o_ref.dtype)


def _matmul_cast(a, w, layer, *, tm, tn, out_dtype, epilogue=_epi_none, side=None,
                 side_rows=None, name):
    m, kdim = a.shape
    n = w.shape[2]
    ni, nj = m // tm, n // tn
    rows = kdim // ni
    row_blk = lambda j, i: jnp.where(j > 0, i, 0)
    col_blk = lambda j: jnp.maximum(j - 1, 0)
    blocks = [((tm, kdim), a.dtype), ((rows, tn), w.dtype), ((tm, tn), out_dtype)]
    scratch_blocks = [((2, kdim, tn), BF16), ((tm, tn), F32)]
    in_specs = [pl.BlockSpec((tm, kdim), lambda j, i: (row_blk(j, i), 0)),
                pl.BlockSpec((None, rows, tn),
                             lambda j, i: (layer, jnp.where(j < nj, i, ni - 1),
                                           jnp.minimum(j, nj - 1)))]
    out_specs = pl.BlockSpec((tm, tn), lambda j, i: (row_blk(j, i), col_blk(j)))
    out_shape = jax.ShapeDtypeStruct((m, n), out_dtype)
    args = [a, w]
    if side is not None:
        _, r, c = side.shape
        n_side = r // side_rows
        assert n_side * side_rows == r and n_side <= (nj + 1) * ni
        side_blk = lambda j, i: jnp.minimum(j * ni + i, n_side - 1)
        in_specs.append(pl.BlockSpec((None, side_rows, c), lambda j, i: (layer, side_blk(j, i), 0)))
        out_specs = [out_specs, pl.BlockSpec((side_rows, c), lambda j, i: (side_blk(j, i), 0))]
        out_shape = [out_shape, jax.ShapeDtypeStruct((r, c), BF16)]
        blocks += [((side_rows, c), F32), ((side_rows, c), BF16)]
        args.append(side)
    return pl.pallas_call(
        functools.partial(_mm_cast_kernel, epilogue=epilogue, has_side=side is not None),
        grid=(nj + 1, ni),
        in_specs=in_specs,
        out_specs=out_specs,
        out_shape=out_shape,
        scratch_shapes=[pltpu.VMEM((2, kdim, tn), BF16)],
        compiler_params=_params(("arbitrary", "arbitrary"), blocks, scratch_blocks),
        name=name,
    )(*args)


def _rope_kernel(pos_ref, inv_ref, cos_ref, sin_ref):
    ang = pos_ref[...].astype(F32) * inv_ref[...]
    cos_ref[...] = jnp.cos(ang)
    sin_ref[...] = jnp.sin(ang)


def _rope_tables(positions, *, tm=1024):
    t = positions.size
    half = RET_HEAD_DIM // 2
    inv = ROPE_BASE ** (-np.arange(0, RET_HEAD_DIM, 2, dtype=np.float64) / RET_HEAD_DIM)
    inv = jnp.asarray(inv.astype(np.float32)).reshape(1, half)
    blocks = [((tm, 1), jnp.int32), ((1, half), F32), ((tm, half), F32), ((tm, half), F32)]
    return pl.pallas_call(
        _rope_kernel,
        grid=(t // tm,),
        in_specs=[pl.BlockSpec((tm, 1), lambda i: (i, 0)),
                  pl.BlockSpec((1, half), lambda i: (0, 0))],
        out_specs=[pl.BlockSpec((tm, half), lambda i: (i, 0))] * 2,
        out_shape=[jax.ShapeDtypeStruct((t, half), F32)] * 2,
        compiler_params=_params(("parallel",), blocks),
        name="rope_tables",
    )(positions.reshape(t, 1), inv)


def _s5_prep_kernel(lr_ref, li_ref, ldt_ref, br_ref, bi_ref, cr_ref, ci_ref,
                    bblk_ref, cblk_ref, pn_ref, pp_ref):
    w = S5_TILE_STATES
    lr = lr_ref[...]
    li = li_ref[...]
    dt = jnp.exp(ldt_ref[...])
    a = lr * dt
    th = li * dt
    mag = jnp.exp(a)
    lbr = mag * jnp.cos(th)
    lbi = mag * jnp.sin(th)
    den = lr * lr + li * li
    nr = lbr - 1.0
    coef_r = (nr * lr + lbi * li) / den
    coef_i = (lbi * lr - nr * li) / den
    br = br_ref[...]
    bi = bi_ref[...]
    bbar_r = coef_r * br - coef_i * bi
    bbar_i = coef_r * bi + coef_i * br

    rows = lax.broadcasted_iota(jnp.int32, (V7X_LANES, w), 0)
    cols = lax.broadcasted_iota(jnp.int32, (V7X_LANES, w), 1)
    same_group = ((rows >> (SSM_GROUP.bit_length() - 1))
                  == (cols >> (SSM_STATE.bit_length() - 1)))
    reps = (S5_GROUPS_PER_TILE, 1)

    def blockdiag(v):
        return jnp.where(same_group, jnp.tile(v, reps), 0.0)

    bblk_ref[:, :w] = blockdiag(bbar_r).astype(bblk_ref.dtype)
    bblk_ref[:, w:] = blockdiag(bbar_i).astype(bblk_ref.dtype)
    cblk_ref[:, :w] = blockdiag(cr_ref[...]).astype(cblk_ref.dtype)
    cblk_ref[:, w:] = blockdiag(-ci_ref[...]).astype(cblk_ref.dtype)

    m = (lax.broadcasted_iota(jnp.int32, (S5_CHUNK, w), 0) + 1).astype(F32)
    cs = jnp.cos(m * th)
    sn = jnp.sin(m * th)
    grow = jnp.exp(m * a)
    shrink = jnp.exp(-(m * a))
    pp_ref[:, :w] = grow * cs
    pp_ref[:, w:] = grow * sn
    pn_ref[:, :w] = shrink * cs
    pn_ref[:, w:] = -(shrink * sn)


def _s5_prep(lam_re, lam_im, log_dt, b_re, b_im, c_re, c_im):
    g, p, h = b_re.shape
    gl, w = S5_GROUPS_PER_TILE, S5_TILE_STATES
    nt = g // gl
    row = lambda v: v.reshape(nt, 1, w)
    bt = lambda v: v.reshape(nt, gl, p, h).transpose(0, 3, 1, 2).reshape(nt, h, w)
    ct = lambda v: v.reshape(nt, gl, h, p).transpose(0, 2, 1, 3).reshape(nt, h, w)
    row_spec = pl.BlockSpec((None, 1, w), lambda j: (j, 0, 0))
    mat_spec = pl.BlockSpec((None, h, w), lambda j: (j, 0, 0))
    blk_spec = pl.BlockSpec((None, V7X_LANES, 2 * w), lambda j: (j, 0, 0))
    pow_spec = pl.BlockSpec((None, S5_CHUNK, 2 * w), lambda j: (j, 0, 0))
    blocks = ([((1, w), F32)] * 3 + [((h, w), F32)] * 4 + [((V7X_LANES, 2 * w), BF16)] * 2
              + [((S5_CHUNK, 2 * w), F32)] * 2)
    return pl.pallas_call(
        _s5_prep_kernel,
        grid=(nt,),
        in_specs=[row_spec] * 3 + [mat_spec] * 4,
        out_specs=[blk_spec, blk_spec, pow_spec, pow_spec],
        out_shape=[jax.ShapeDtypeStruct((nt, V7X_LANES, 2 * w), BF16)] * 2
        + [jax.ShapeDtypeStruct((nt, S5_CHUNK, 2 * w), F32)] * 2,
        compiler_params=_params(("parallel",), blocks),
        name="s5_prep",
    )(row(lam_re), row(lam_im), row(jnp.repeat(log_dt, p)), bt(b_re), bt(b_im), ct(c_re), ct(c_im))


def _s5_kernel(u_ref, bblk_ref, cblk_ref, pn_ref, pp_ref, d_ref, y_ref, *, n_chunks):
    w = S5_TILE_STATES
    lc = S5_CHUNK
    bblk = bblk_ref[...]
    cblk = cblk_ref[...]
    d = d_ref[...]
    r = lax.broadcasted_iota(jnp.int32, (lc, lc), 0)
    c = lax.broadcasted_iota(jnp.int32, (lc, lc), 1)
    tri = (r >= c).astype(BF16)

    def chunk(t, carry):
        cr, ci = carry
        rows = pl.ds(pl.multiple_of(t * lc, lc), lc)
        u = u_ref[rows, :]
        bu = jnp.dot(u.astype(BF16), bblk, preferred_element_type=F32)
        br, bi = bu[:, :w], bu[:, w:]
        pnr, pni = pn_ref[:, :w], pn_ref[:, w:]
        z = jnp.concatenate([pnr * br - pni * bi, pnr * bi + pni * br], axis=1)
        ws = jnp.dot(tri, z.astype(BF16), preferred_element_type=F32)
        wr = ws[:, :w] + cr
        wi = ws[:, w:] + ci
        ppr, ppi = pp_ref[:, :w], pp_ref[:, w:]
        xr = ppr * wr - ppi * wi
        xi = ppr * wi + ppi * wr
        x = jnp.concatenate([xr, xi], axis=1).astype(BF16)
        y = lax.dot_general(x, cblk, (((1,), (1,)), ((), ())), preferred_element_type=F32)
        y_ref[rows, :] = jax.nn.gelu(y + d * u)
        return xr[lc - 1:, :], xi[lc - 1:, :]

    zero = jnp.zeros((1, w), F32)
    lax.fori_loop(0, n_chunks, chunk, (zero, zero), unroll=4)


def _s5_scan(proj, tables, d_skip, bsz, seq):
    bblk, cblk, pn, pp = tables
    nt = bblk.shape[0]
    w2 = 2 * S5_TILE_STATES
    tab = lambda rows: pl.BlockSpec((None, rows, w2), lambda j, b: (j, 0, 0))
    blocks = [((seq, V7X_LANES), F32), ((V7X_LANES, w2), BF16), ((V7X_LANES, w2), BF16),
              ((S5_CHUNK, w2), F32), ((S5_CHUNK, w2), F32), ((1, V7X_LANES), F32),
              ((seq, V7X_LANES), F32)]
    return pl.pallas_call(
        functools.partial(_s5_kernel, n_chunks=seq // S5_CHUNK),
        grid=(nt, bsz),
        in_specs=[pl.BlockSpec((seq, V7X_LANES), lambda j, b: (b, j)),
                  tab(V7X_LANES), tab(V7X_LANES), tab(S5_CHUNK), tab(S5_CHUNK),
                  pl.BlockSpec((1, V7X_LANES), lambda j, b: (0, j))],
        out_specs=pl.BlockSpec((seq, V7X_LANES), lambda j, b: (b, j)),
        out_shape=jax.ShapeDtypeStruct((bsz * seq, SSM_WIDTH), F32),
        compiler_params=_params(("parallel", "parallel"), blocks),
        name="s5_scan",
    )(proj, bblk, cblk, pn, pp, d_skip.reshape(1, SSM_WIDTH))


def _glu_kernel(y_ref, w_ref, b_ref, g_ref, o_ref):
    y = y_ref[...]
    z = jnp.dot(y.astype(BF16), w_ref[...], preferred_element_type=F32) + b_ref[...]
    v = y * jax.nn.sigmoid(z)
    ms = jnp.mean(v * v, axis=-1, keepdims=True)
    o_ref[...] = (v * lax.rsqrt(ms + EPS) * g_ref[...]).astype(o_ref.dtype)


def _glu_norm(y, w_glu, layer, b_glu, g_out, *, tm=512):
    t, n = y.shape
    blocks = [((tm, n), F32), ((n, n), BF16), ((1, n), F32), ((1, n), F32), ((tm, n), BF16)]
    vec = pl.BlockSpec((1, n), lambda i: (0, 0))
    return pl.pallas_call(
        _glu_kernel,
        grid=(t // tm,),
        in_specs=[pl.BlockSpec((tm, n), lambda i: (i, 0)),
                  pl.BlockSpec((None, n, n), lambda i: (layer, 0, 0)), vec, vec],
        out_specs=pl.BlockSpec((tm, n), lambda i: (i, 0)),
        out_shape=jax.ShapeDtypeStruct((t, n), BF16),
        compiler_params=_params(("parallel",), blocks),
        name="glu_norm",
    )(y, w_glu, b_glu.reshape(1, n), g_out.reshape(1, n))


def _retention_tables():
    h, c, dh = RET_HEADS, RET_CHUNK, RET_HEAD_DIM
    log_gamma = np.log1p(-np.exp2(-5.0 - np.arange(h, dtype=np.float64)))
    idx = np.arange(c, dtype=np.float64)
    rel = idx[:, None] - idx[None, :]
    mask = np.where(rel >= 0, np.exp(log_gamma[:, None, None] * np.maximum(rel, 0.0)), 0.0)
    cross = np.exp(log_gamma[:, None] * (idx + 1.0))
    state = np.exp(log_gamma[:, None] * (c - 1.0 - idx))
    chunk = np.exp(log_gamma * c)
    wide = lambda v: np.broadcast_to(v[:, :, None], (h, c, dh))
    f = lambda v: jnp.asarray(np.ascontiguousarray(v).astype(np.float32))
    return f(mask), f(wide(cross)), f(wide(state)), f(np.broadcast_to(chunk[:, None, None], (h, 1, dh)))


def _retention_kernel(q_ref, k_ref, v_ref, g_ref, cos_ref, sin_ref, mask_ref, cd_ref, sd_ref,
                      ch_ref, gn_ref, o_ref, state_ref, *, chunks_per_step):
    half = RET_HEAD_DIM // 2
    c = RET_CHUNK

    @pl.when(pl.program_id(2) == 0)
    def _():
        state_ref[...] = jnp.zeros_like(state_ref)

    mask = mask_ref[...]
    cross_decay = cd_ref[...]
    state_decay = sd_ref[...]
    chunk_decay = ch_ref[...]
    gn = gn_ref[...]
    k_scale = RET_HEAD_DIM ** -0.5

    def rotate(t, cos, sin):
        t1, t2 = t[:, :half], t[:, half:]
        return jnp.concatenate([t1 * cos - t2 * sin, t1 * sin + t2 * cos], axis=1)

    for ci in range(chunks_per_step):
        rows = pl.ds(ci * c, c)
        cos = cos_ref[rows, :]
        sin = sin_ref[rows, :]
        q = rotate(q_ref[rows, :], cos, sin)
        k = rotate(k_ref[rows, :], cos, sin) * k_scale
        v = v_ref[rows, :].astype(BF16)
        qb = q.astype(BF16)
        state = state_ref[...]
        scores = lax.dot_general(qb, k.astype(BF16), (((1,), (1,)), ((), ())),
                                 preferred_element_type=F32) * mask
        inner = jnp.dot(scores.astype(BF16), v, preferred_element_type=F32)
        cross = jnp.dot(qb, state.astype(BF16), preferred_element_type=F32) * cross_decay
        kv = lax.dot_general((k * state_decay).astype(BF16), v, (((0,), (0,)), ((), ())),
                             preferred_element_type=F32)
        state_ref[...] = state * chunk_decay + kv
        out = inner + cross
        mu = jnp.mean(out, axis=-1, keepdims=True)
        dev = out - mu
        var = jnp.mean(dev * dev, axis=-1, keepdims=True)
        g = g_ref[rows, :]
        o_ref[rows, :] = (dev * lax.rsqrt(var + EPS) * gn * (g * jax.nn.sigmoid(g))).astype(o_ref.dtype)


def _retention(proj, cos, sin, g_norm, bsz, seq, *, chunks_per_step=8):
    dh, h = RET_HEAD_DIM, RET_HEADS
    rb = chunks_per_step * RET_CHUNK
    steps = seq // rb
    mask, cd, sd, ch = _retention_tables()
    first = SSM_WIDTH // dh
    col = lambda which: pl.BlockSpec(
        (rb, dh), lambda b, hh, s, which=which: (b * steps + s, first + which * h + hh))
    rope = pl.BlockSpec((rb, dh // 2), lambda b, hh, s: (b * steps + s, 0))
    per_head = lambda rows: pl.BlockSpec((None, rows, dh), lambda b, hh, s: (hh, 0, 0))
    blocks = ([((rb, dh), F32)] * 4 + [((rb, dh // 2), F32)] * 2
              + [((RET_CHUNK, RET_CHUNK), F32), ((RET_CHUNK, dh), F32), ((RET_CHUNK, dh), F32),
                 ((1, dh), F32), ((1, dh), F32), ((rb, dh), BF16)])
    return pl.pallas_call(
        functools.partial(_retention_kernel, chunks_per_step=chunks_per_step),
        grid=(bsz, h, steps),
        in_specs=[col(0), col(1), col(2), col(3), rope, rope,
                  pl.BlockSpec((None, RET_CHUNK, RET_CHUNK), lambda b, hh, s: (hh, 0, 0)),
                  per_head(RET_CHUNK), per_head(RET_CHUNK), per_head(1),
                  pl.BlockSpec((1, dh), lambda b, hh, s: (0, hh))],
        out_specs=pl.BlockSpec((rb, dh), lambda b, hh, s: (b * steps + s, hh)),
        out_shape=jax.ShapeDtypeStruct((bsz * seq, RET_WIDTH), BF16),
        scratch_shapes=[pltpu.VMEM((dh, dh), F32)],
        compiler_params=_params(("parallel", "parallel", "arbitrary"), blocks, [((dh, dh), F32)]),
        name="retention",
    )(proj, proj, proj, proj, cos, sin, mask, cd, sd, ch, g_norm.reshape(1, RET_WIDTH))


def kernel(x, c, positions, w_ada, b_ada, g_mix, w_in, lam_re, lam_im, log_dt, b_re, b_im, c_re,
           c_im, d_skip, w_glu, b_glu, g_ssm_out, g_ret_norm, w_out, g_mlp, w_up, w_down, g_final):
    bsz, seq, d = x.shape
    depth = w_ada.shape[0]
    x2 = x.reshape(bsz * seq, d)
    c_pad = jnp.zeros((ADA_ROWS, d), F32).at[:bsz].set(c)
    b_ada3 = b_ada.reshape(depth, 1, b_ada.shape[1])
    mod = _ada_mod(c_pad, w_ada, b_ada3, 0)
    cos, sin = _rope_tables(positions)
    w_glu_b = w_glu.astype(BF16)
    for i in range(depth):
        shift1, scale1, gate1, shift2, scale2, gate2 = jnp.split(mod[:bsz], 6, axis=-1)

        h = _normmod(x2, g_mix[i], scale1, shift1, seq)
        proj, w_out_b = _matmul_cast(h, w_in, i, tm=1024, tn=1024, out_dtype=F32,
                                     side=w_out, side_rows=64, name="in_proj")
        tables = _s5_prep(lam_re[i], lam_im[i], log_dt[i], b_re[i], b_im[i], c_re[i], c_im[i])
        y = _s5_scan(proj, tables, d_skip[i], bsz, seq)
        ssm_out = _glu_norm(y, w_glu_b, i, b_glu[i], g_ssm_out[i])
        ret_out = _retention(proj, cos, sin, g_ret_norm[i], bsz, seq)
        x2 = _matmul([ssm_out, ret_out], w_out_b[None], 0, tm=1024, tn=1024, tk=d, out_dtype=F32,
                     epilogue=_epi_resid, resid=x2, gate=gate1, seq=seq, name="out_proj")

        h = _normmod(x2, g_mlp[i], scale2, shift2, seq)
        a, w_down_b = _matmul_cast(h, w_up, i, tm=1024, tn=1024, out_dtype=BF16,
                                   epilogue=_epi_relu2, side=w_down, side_rows=128, name="mlp_up")
        ada = (c_pad, w_ada, b_ada3, i + 1) if i + 1 < depth else None
        res = _matmul([a], w_down_b[None], 0, tm=1024, tn=1024, tk=2048, out_dtype=F32,
                      epilogue=_epi_resid, resid=x2, gate=gate2, seq=seq, ada=ada, name="mlp_down")
        x2, mod = res if ada is not None else (res, None)
    return _rmsnorm(x2, g_final).reshape(bsz, seq, d)
```

```python
import functools

import numpy as np
import jax
import jax.numpy as jnp
from jax import lax
from jax.experimental import pallas as pl
from jax.experimental.pallas import tpu as pltpu

F32 = jnp.float32
BF16 = jnp.bfloat16

SSM_WIDTH = 2048
RET_WIDTH = 2048
SSM_GROUP = 16
SSM_STATE = 64
RET_HEAD_DIM = 256
RET_HEADS = 8
RET_CHUNK = 128
ROPE_BASE = 10000.0
EPS = 1e-6

V7X_LANES = 128
V7X_VMEM_BYTES = 64 * 1024 * 1024
V7X_COMPILER_SCRATCH_BYTES = 6 * 1024 * 1024

S5_CHUNK = 256
S5_GROUPS_PER_TILE = V7X_LANES // SSM_GROUP
S5_TILE_STATES = S5_GROUPS_PER_TILE * SSM_STATE


def _nbytes(shape, dtype):
    return int(np.prod(shape)) * jnp.dtype(dtype).itemsize


def _params(semantics, pipelined_blocks, scratch_blocks=()):
    need = 2 * sum(_nbytes(s, d) for s, d in pipelined_blocks)
    need += sum(_nbytes(s, d) for s, d in scratch_blocks)
    limit = min(need + V7X_COMPILER_SCRATCH_BYTES, V7X_VMEM_BYTES - 4 * 1024 * 1024)
    return pltpu.CompilerParams(dimension_semantics=semantics, vmem_limit_bytes=limit)


def _ada_kernel(c_ref, w_ref, b_ref, o_ref):
    c = c_ref[...]
    c_act = c * jax.nn.sigmoid(c)
    o_ref[...] = jnp.dot(c_act.astype(BF16), w_ref[...].astype(BF16),
                         preferred_element_type=F32) + b_ref[...]


ADA_ROWS = 8


def _ada_mod(c_pad, w_ada, b_ada3, layer, *, tn=1024):
    _, d, n = w_ada.shape
    rows = ADA_ROWS
    blocks = [((rows, d), F32), ((d, tn), F32), ((1, tn), F32), ((rows, tn), F32)]
    return pl.pallas_call(
        _ada_kernel,
        grid=(n // tn,),
        in_specs=[pl.BlockSpec((rows, d), lambda j: (0, 0)),
                  pl.BlockSpec((None, d, tn), lambda j: (layer, 0, j)),
                  pl.BlockSpec((None, 1, tn), lambda j: (layer, 0, j))],
        out_specs=pl.BlockSpec((rows, tn), lambda j: (0, j)),
        out_shape=jax.ShapeDtypeStruct((rows, n), F32),
        compiler_params=_params(("parallel",), blocks),
        name="ada_mod",
    )(c_pad, w_ada, b_ada3)


def _normmod_kernel(x_ref, g_ref, sc_ref, sh_ref, o_ref):
    x = x_ref[...]
    ms = jnp.mean(x * x, axis=-1, keepdims=True)
    y = x * lax.rsqrt(ms + EPS) * g_ref[...]
    o_ref[...] = (y * (1.0 + sc_ref[...]) + sh_ref[...]).astype(o_ref.dtype)


def _normmod(x2, g, scale, shift, seq, *, tm=512):
    t, d = x2.shape
    bsz = scale.shape[0]
    blocks = [((tm, d), F32), ((1, d), F32), ((1, d), F32), ((1, d), F32), ((tm, d), BF16)]
    per_b = pl.BlockSpec((None, 1, d), lambda i: (i * tm // seq, 0, 0))
    return pl.pallas_call(
        _normmod_kernel,
        grid=(t // tm,),
        in_specs=[pl.BlockSpec((tm, d), lambda i: (i, 0)),
                  pl.BlockSpec((1, d), lambda i: (0, 0)), per_b, per_b],
        out_specs=pl.BlockSpec((tm, d), lambda i: (i, 0)),
        out_shape=jax.ShapeDtypeStruct((t, d), BF16),
        compiler_params=_params(("parallel",), blocks, [((tm, d), F32)]),
        name="normmod",
    )(x2, g.reshape(1, d), scale.reshape(bsz, 1, d), shift.reshape(bsz, 1, d))


def _rmsnorm_kernel(x_ref, g_ref, o_ref):
    x = x_ref[...]
    ms = jnp.mean(x * x, axis=-1, keepdims=True)
    o_ref[...] = x * lax.rsqrt(ms + EPS) * g_ref[...]


def _rmsnorm(x2, g, *, tm=512):
    t, d = x2.shape
    blocks = [((tm, d), F32), ((1, d), F32), ((tm, d), F32)]
    return pl.pallas_call(
        _rmsnorm_kernel,
        grid=(t // tm,),
        in_specs=[pl.BlockSpec((tm, d), lambda i: (i, 0)),
                  pl.BlockSpec((1, d), lambda i: (0, 0))],
        out_specs=pl.BlockSpec((tm, d), lambda i: (i, 0)),
        out_shape=jax.ShapeDtypeStruct((t, d), F32),
        compiler_params=_params(("parallel",), blocks, [((tm, d), F32)]),
        name="final_rmsnorm",
    )(x2, g.reshape(1, d))


def _epi_none(acc):
    return acc


def _epi_relu2(acc):
    r = jnp.maximum(acc, 0.0)
    return r * r


def _epi_resid(acc, x_ref, gate_ref):
    return x_ref[...] + gate_ref[...] * acc


def _mm_kernel(*refs, n_a, nk, epilogue, has_ada):
    a_refs, w_ref, rest = refs[:n_a], refs[n_a], refs[n_a + 1:]
    if has_ada:
        n_main = len(rest) - (5 if nk == 1 else 6)
        c_ref, wa_ref, ba_ref = rest[n_main:n_main + 3]
        mod_ref = rest[n_main + 4]
        _ada_kernel(c_ref, wa_ref, ba_ref, mod_ref)
        rest = rest[:n_main] + (rest[n_main + 3],) + rest[n_main + 5:]

    def product():
        total, row = None, 0
        for a_ref in a_refs:
            kk = a_ref.shape[1]
            part = jnp.dot(a_ref[...], w_ref[row:row + kk, :], preferred_element_type=F32)
            total = part if total is None else total + part
            row += kk
        return total

    if nk == 1:
        *extra, o_ref = rest
        o_ref[...] = epilogue(product(), *extra).astype(o_ref.dtype)
        return
    *extra, o_ref, acc_ref = rest
    k = pl.program_id(2)

    @pl.when(k == 0)
    def _():
        acc_ref[...] = jnp.zeros_like(acc_ref)

    acc_ref[...] += product()

    @pl.when(k == nk - 1)
    def _():
        o_ref[...] = epilogue(acc_ref[...], *extra).astype(o_ref.dtype)


def _matmul(a_list, w, layer, *, tm, tn, tk, out_dtype, epilogue=_epi_none, resid=None,
            gate=None, seq=None, ada=None, name):
    m = a_list[0].shape[0]
    _, kdim, n = w.shape
    nk = kdim // tk
    if len(a_list) == 1:
        in_specs = [pl.BlockSpec((tm, tk), lambda i, j, k: (i, k))]
        blocks = [((tm, tk), a_list[0].dtype)]
    else:
        assert nk == 1 and sum(a.shape[1] for a in a_list) == kdim
        in_specs = [pl.BlockSpec((tm, a.shape[1]), lambda i, j, k: (i, 0)) for a in a_list]
        blocks = [((tm, a.shape[1]), a.dtype) for a in a_list]
    in_specs.append(pl.BlockSpec((None, tk, tn), lambda i, j, k: (layer, k, j)))
    blocks += [((tk, tn), w.dtype), ((tm, tn), out_dtype)]
    args = list(a_list) + [w]
    if resid is not None:
        bsz = gate.shape[0]
        in_specs += [pl.BlockSpec((tm, tn), lambda i, j, k: (i, j)),
                     pl.BlockSpec((None, 1, tn), lambda i, j, k: (i * tm // seq, 0, j))]
        blocks += [((tm, tn), F32), ((1, tn), F32)]
        args += [resid, gate.reshape(bsz, 1, n)]
    scratch = [] if nk == 1 else [pltpu.VMEM((tm, tn), F32)]
    scratch_blocks = [((tm, tn), F32)] * (1 if nk == 1 else 2)
    out_specs = pl.BlockSpec((tm, tn), lambda i, j, k: (i, j))
    out_shape = jax.ShapeDtypeStruct((m, n), out_dtype)
    if ada is not None:
        c_pad, w_ada, b_ada3, ada_layer = ada
        _, d_ada, n_ada = w_ada.shape
        n_blk = n_ada // V7X_LANES
        nj = n // tn
        assert n_blk <= (m // tm) * nj * nk
        blk = lambda i, j, k: jnp.minimum((i * nj + j) * nk + k, n_blk - 1)
        in_specs += [pl.BlockSpec((ADA_ROWS, d_ada), lambda i, j, k: (0, 0)),
                     pl.BlockSpec((None, d_ada, V7X_LANES), lambda i, j, k: (ada_layer, 0, blk(i, j, k))),
                     pl.BlockSpec((None, 1, V7X_LANES), lambda i, j, k: (ada_layer, 0, blk(i, j, k)))]
        out_specs = [out_specs, pl.BlockSpec((ADA_ROWS, V7X_LANES), lambda i, j, k: (0, blk(i, j, k)))]
        out_shape = [out_shape, jax.ShapeDtypeStruct((ADA_ROWS, n_ada), F32)]
        blocks += [((ADA_ROWS, d_ada), F32), ((d_ada, V7X_LANES), F32), ((1, V7X_LANES), F32),
                   ((ADA_ROWS, V7X_LANES), F32)]
        args += [c_pad, w_ada, b_ada3]
    return pl.pallas_call(
        functools.partial(_mm_kernel, n_a=len(a_list), nk=nk, epilogue=epilogue,
                          has_ada=ada is not None),
        grid=(m // tm, n // tn, nk),
        in_specs=in_specs,
        out_specs=out_specs,
        out_shape=out_shape,
        scratch_shapes=scratch,
        compiler_params=_params(("arbitrary",) * 3 if ada is not None
                                else ("parallel", "parallel", "arbitrary"), blocks, scratch_blocks),
        name=name,
    )(*args)


def _mm_cast_kernel(a_ref, wc_ref, *rest, epilogue, has_side):
    if has_side:
        side_ref, o_ref, side_o_ref, wb_ref = rest
        side_o_ref[...] = side_ref[...].astype(BF16)
    else:
        o_ref, wb_ref = rest
    j = pl.program_id(0)
    i = pl.program_id(1)
    nj = pl.num_programs(0) - 1
    rows = wc_ref.shape[0]

    @pl.when(j < nj)
    def _():
        wb_ref[j % 2, pl.ds(pl.multiple_of(i * rows, rows), rows), :] = wc_ref[...].astype(BF16)

    @pl.when(j > 0)
    def _():
        prod = jnp.dot(a_ref[...], wb_ref[(j - 1) % 2], preferred_element_type=F32)
        o_ref[...] = epilogue(prod).astype(o_ref.dtype)


def _matmul_cast(a, w, layer, *, tm, tn, out_dtype, epilogue=_epi_none, side=None,
                 side_rows=None, name):
    m, kdim = a.shape
    n = w.shape[2]
    ni, nj = m // tm, n // tn
    rows = kdim // ni
    row_blk = lambda j, i: jnp.where(j > 0, i, 0)
    col_blk = lambda j: jnp.maximum(j - 1, 0)
    blocks = [((tm, kdim), a.dtype), ((rows, tn), w.dtype), ((tm, tn), out_dtype)]
    scratch_blocks = [((2, kdim, tn), BF16), ((tm, tn), F32)]
    in_specs = [pl.BlockSpec((tm, kdim), lambda j, i: (row_blk(j, i), 0)),
                pl.BlockSpec((None, rows, tn),
                             lambda j, i: (layer, jnp.where(j < nj, i, ni - 1),
                                           jnp.minimum(j, nj - 1)))]
    out_specs = pl.BlockSpec((tm, tn), lambda j, i: (row_blk(j, i), col_blk(j)))
    out_shape = jax.ShapeDtypeStruct((m, n), out_dtype)
    args = [a, w]
    if side is not None:
        _, r, c = side.shape
        n_side = r // side_rows
        assert n_side * side_rows == r and n_side <= (nj + 1) * ni
        side_blk = lambda j, i: jnp.minimum(j * ni + i, n_side - 1)
        in_specs.append(pl.BlockSpec((None, side_rows, c), lambda j, i: (layer, side_blk(j, i), 0)))
        out_specs = [out_specs, pl.BlockSpec((side_rows, c), lambda j, i: (side_blk(j, i), 0))]
        out_shape = [out_shape, jax.ShapeDtypeStruct((r, c), BF16)]
        blocks += [((side_rows, c), F32), ((side_rows, c), BF16)]
        args.append(side)
    return pl.pallas_call(
        functools.partial(_mm_cast_kernel, epilogue=epilogue, has_side=side is not None),
        grid=(nj + 1, ni),
        in_specs=in_specs,
        out_specs=out_specs,
        out_shape=out_shape,
        scratch_shapes=[pltpu.VMEM((2, kdim, tn), BF16)],
        compiler_params=_params(("arbitrary", "arbitrary"), blocks, scratch_blocks),
        name=name,
    )(*args)


def _rope_kernel(pos_ref, inv_ref, cos_ref, sin_ref):
    ang = pos_ref[...].astype(F32) * inv_ref[...]
    cos_ref[...] = jnp.cos(ang)
    sin_ref[...] = jnp.sin(ang)


def _rope_tables(positions, *, tm=1024):
    t = positions.size
    half = RET_HEAD_DIM // 2
    inv = ROPE_BASE ** (-np.arange(0, RET_HEAD_DIM, 2, dtype=np.float64) / RET_HEAD_DIM)
    inv = jnp.asarray(inv.astype(np.float32)).reshape(1, half)
    blocks = [((tm, 1), jnp.int32), ((1, half), F32), ((tm, half), F32), ((tm, half), F32)]
    return pl.pallas_call(
        _rope_kernel,
        grid=(t // tm,),
        in_specs=[pl.BlockSpec((tm, 1), lambda i: (i, 0)),
                  pl.BlockSpec((1, half), lambda i: (0, 0))],
        out_specs=[pl.BlockSpec((tm, half), lambda i: (i, 0))] * 2,
        out_shape=[jax.ShapeDtypeStruct((t, half), F32)] * 2,
        compiler_params=_params(("parallel",), blocks),
        name="rope_tables",
    )(positions.reshape(t, 1), inv)


def _s5_prep_kernel(lr_ref, li_ref, ldt_ref, br_ref, bi_ref, cr_ref, ci_ref,
                    bblk_ref, cblk_ref, pn_ref, pp_ref):
    w = S5_TILE_STATES
    lr = lr_ref[...]
    li = li_ref[...]
    dt = jnp.exp(ldt_ref[...])
    a = lr * dt
    th = li * dt
    mag = jnp.exp(a)
    lbr = mag * jnp.cos(th)
    lbi = mag * jnp.sin(th)
    den = lr * lr + li * li
    nr = lbr - 1.0
    coef_r = (nr * lr + lbi * li) / den
    coef_i = (lbi * lr - nr * li) / den
    br = br_ref[...]
    bi = bi_ref[...]
    bbar_r = coef_r * br - coef_i * bi
    bbar_i = coef_r * bi + coef_i * br

    rows = lax.broadcasted_iota(jnp.int32, (V7X_LANES, w), 0)
    cols = lax.broadcasted_iota(jnp.int32, (V7X_LANES, w), 1)
    same_group = ((rows >> (SSM_GROUP.bit_length() - 1))
                  == (cols >> (SSM_STATE.bit_length() - 1)))
    reps = (S5_GROUPS_PER_TILE, 1)

    def blockdiag(v):
        return jnp.where(same_group, jnp.tile(v, reps), 0.0)

    bblk_ref[:, :w] = blockdiag(bbar_r).astype(bblk_ref.dtype)
    bblk_ref[:, w:] = blockdiag(bbar_i).astype(bblk_ref.dtype)
    cblk_ref[:, :w] = blockdiag(cr_ref[...]).astype(cblk_ref.dtype)
    cblk_ref[:, w:] = blockdiag(-ci_ref[...]).astype(cblk_ref.dtype)

    m = (lax.broadcasted_iota(jnp.int32, (S5_CHUNK, w), 0) + 1).astype(F32)
    cs = jnp.cos(m * th)
    sn = jnp.sin(m * th)
    grow = jnp.exp(m * a)
    shrink = jnp.exp(-(m * a))
    pp_ref[:, :w] = grow * cs
    pp_ref[:, w:] = grow * sn
    pn_ref[:, :w] = shrink * cs
    pn_ref[:, w:] = -(shrink * sn)


def _s5_prep(lam_re, lam_im, log_dt, b_re, b_im, c_re, c_im):
    g, p, h = b_re.shape
    gl, w = S5_GROUPS_PER_TILE, S5_TILE_STATES
    nt = g // gl
    row = lambda v: v.reshape(nt, 1, w)
    bt = lambda v: v.reshape(nt, gl, p, h).transpose(0, 3, 1, 2).reshape(nt, h, w)
    ct = lambda v: v.reshape(nt, gl, h, p).transpose(0, 2, 1, 3).reshape(nt, h, w)
    row_spec = pl.BlockSpec((None, 1, w), lambda j: (j, 0, 0))
    mat_spec = pl.BlockSpec((None, h, w), lambda j: (j, 0, 0))
    blk_spec = pl.BlockSpec((None, V7X_LANES, 2 * w), lambda j: (j, 0, 0))
    pow_spec = pl.BlockSpec((None, S5_CHUNK, 2 * w), lambda j: (j, 0, 0))
    blocks = ([((1, w), F32)] * 3 + [((h, w), F32)] * 4 + [((V7X_LANES, 2 * w), BF16)] * 2
              + [((S5_CHUNK, 2 * w), F32)] * 2)
    return pl.pallas_call(
        _s5_prep_kernel,
        grid=(nt,),
        in_specs=[row_spec] * 3 + [mat_spec] * 4,
        out_specs=[blk_spec, blk_spec, pow_spec, pow_spec],
        out_shape=[jax.ShapeDtypeStruct((nt, V7X_LANES, 2 * w), BF16)] * 2
        + [jax.ShapeDtypeStruct((nt, S5_CHUNK, 2 * w), F32)] * 2,
        compiler_params=_params(("parallel",), blocks),
        name="s5_prep",
    )(row(lam_re), row(lam_im), row(jnp.repeat(log_dt, p)), bt(b_re), bt(b_im), ct(c_re), ct(c_im))


def _s5_kernel(u_ref, bblk_ref, cblk_ref, pn_ref, pp_ref, d_ref, y_ref, *, n_chunks):
    w = S5_TILE_STATES
    lc = S5_CHUNK
    bblk = bblk_ref[...]
    cblk = cblk_ref[...]
    d = d_ref[...]
    r = lax.broadcasted_iota(jnp.int32, (lc, lc), 0)
    c = lax.broadcasted_iota(jnp.int32, (lc, lc), 1)
    tri = (r >= c).astype(BF16)

    def chunk(t, carry):
        cr, ci = carry
        rows = pl.ds(pl.multiple_of(t * lc, lc), lc)
        u = u_ref[rows, :]
        bu = jnp.dot(u.astype(BF16), bblk, preferred_element_type=F32)
        br, bi = bu[:, :w], bu[:, w:]
        pnr, pni = pn_ref[:, :w], pn_ref[:, w:]
        z = jnp.concatenate([pnr * br - pni * bi, pnr * bi + pni * br], axis=1)
        ws = jnp.dot(tri, z.astype(BF16), preferred_element_type=F32)
        wr = ws[:, :w] + cr
        wi = ws[:, w:] + ci
        ppr, ppi = pp_ref[:, :w], pp_ref[:, w:]
        xr = ppr * wr - ppi * wi
        xi = ppr * wi + ppi * wr
        x = jnp.concatenate([xr, xi], axis=1).astype(BF16)
        y = lax.dot_general(x, cblk, (((1,), (1,)), ((), ())), preferred_element_type=F32)
        y_ref[rows, :] = jax.nn.gelu(y + d * u)
        return xr[lc - 1:, :], xi[lc - 1:, :]

    zero = jnp.zeros((1, w), F32)
    lax.fori_loop(0, n_chunks, chunk, (zero, zero), unroll=8)


def _s5_scan(proj, tables, d_skip, bsz, seq):
    bblk, cblk, pn, pp = tables
    nt = bblk.shape[0]
    w2 = 2 * S5_TILE_STATES
    tab = lambda rows: pl.BlockSpec((None, rows, w2), lambda j, b: (j, 0, 0))
    blocks = [((seq, V7X_LANES), F32), ((V7X_LANES, w2), BF16), ((V7X_LANES, w2), BF16),
              ((S5_CHUNK, w2), F32), ((S5_CHUNK, w2), F32), ((1, V7X_LANES), F32),
              ((seq, V7X_LANES), F32)]
    return pl.pallas_call(
        functools.partial(_s5_kernel, n_chunks=seq // S5_CHUNK),
        grid=(nt, bsz),
        in_specs=[pl.BlockSpec((seq, V7X_LANES), lambda j, b: (b, j)),
                  tab(V7X_LANES), tab(V7X_LANES), tab(S5_CHUNK), tab(S5_CHUNK),
                  pl.BlockSpec((1, V7X_LANES), lambda j, b: (0, j))],
        out_specs=pl.BlockSpec((seq, V7X_LANES), lambda j, b: (b, j)),
        out_shape=jax.ShapeDtypeStruct((bsz * seq, SSM_WIDTH), F32),
        compiler_params=_params(("parallel", "parallel"), blocks),
        name="s5_scan",
    )(proj, bblk, cblk, pn, pp, d_skip.reshape(1, SSM_WIDTH))


def _glu_kernel(y_ref, w_ref, b_ref, g_ref, o_ref):
    y = y_ref[...]
    z = jnp.dot(y.astype(BF16), w_ref[...], preferred_element_type=F32) + b_ref[...]
    v = y * jax.nn.sigmoid(z)
    ms = jnp.mean(v * v, axis=-1, keepdims=True)
    o_ref[...] = (v * lax.rsqrt(ms + EPS) * g_ref[...]).astype(o_ref.dtype)


def _glu_norm(y, w_glu, layer, b_glu, g_out, *, tm=512):
    t, n = y.shape
    blocks = [((tm, n), F32), ((n, n), BF16), ((1, n), F32), ((1, n), F32), ((tm, n), BF16)]
    vec = pl.BlockSpec((1, n), lambda i: (0, 0))
    return pl.pallas_call(
        _glu_kernel,
        grid=(t // tm,),
        in_specs=[pl.BlockSpec((tm, n), lambda i: (i, 0)),
                  pl.BlockSpec((None, n, n), lambda i: (layer, 0, 0)), vec, vec],
        out_specs=pl.BlockSpec((tm, n), lambda i: (i, 0)),
        out_shape=jax.ShapeDtypeStruct((t, n), BF16),
        compiler_params=_params(("parallel",), blocks),
        name="glu_norm",
    )(y, w_glu, b_glu.reshape(1, n), g_out.reshape(1, n))


def _retention_tables():
    h, c, dh = RET_HEADS, RET_CHUNK, RET_HEAD_DIM
    log_gamma = np.log1p(-np.exp2(-5.0 - np.arange(h, dtype=np.float64)))
    idx = np.arange(c, dtype=np.float64)
    rel = idx[:, None] - idx[None, :]
    mask = np.where(rel >= 0, np.exp(log_gamma[:, None, None] * np.maximum(rel, 0.0)), 0.0)
    cross = np.exp(log_gamma[:, None] * (idx + 1.0))
    state = np.exp(log_gamma[:, None] * (c - 1.0 - idx))
    chunk = np.exp(log_gamma * c)
    k_scale = dh ** -0.5
    assert np.log2(k_scale) == np.round(np.log2(k_scale))
    mask = mask * k_scale
    state = state * k_scale
    wide = lambda v: np.broadcast_to(v[:, :, None], (h, c, dh))
    f = lambda v: jnp.asarray(np.ascontiguousarray(v).astype(np.float32))
    return f(mask), f(wide(cross)), f(wide(state)), f(np.broadcast_to(chunk[:, None, None], (h, 1, dh)))


def _retention_kernel(q_ref, k_ref, v_ref, g_ref, cos_ref, sin_ref, mask_ref, cd_ref, sd_ref,
                      ch_ref, gn_ref, o_ref, state_ref, *, chunks_per_step):
    half = RET_HEAD_DIM // 2
    c = RET_CHUNK

    @pl.when(pl.program_id(2) == 0)
    def _():
        state_ref[...] = jnp.zeros_like(state_ref)

    mask = mask_ref[...]
    cross_decay = cd_ref[...]
    state_decay = sd_ref[...]
    chunk_decay = ch_ref[...]
    gn = gn_ref[...]

    def rotate(t, cos, sin):
        t1, t2 = t[:, :half], t[:, half:]
        return jnp.concatenate([t1 * cos - t2 * sin, t1 * sin + t2 * cos], axis=1)

    for ci in range(chunks_per_step):
        rows = pl.ds(ci * c, c)
        cos = cos_ref[rows, :]
        sin = sin_ref[rows, :]
        q = rotate(q_ref[rows, :], cos, sin)
        k = rotate(k_ref[rows, :], cos, sin)
        v = v_ref[rows, :].astype(BF16)
        qb = q.astype(BF16)
        state = state_ref[...]
        scores = lax.dot_general(qb, k.astype(BF16), (((1,), (1,)), ((), ())),
                                 preferred_element_type=F32) * mask
        inner = jnp.dot(scores.astype(BF16), v, preferred_element_type=F32)
        cross = jnp.dot(qb, state.astype(BF16), preferred_element_type=F32) * cross_decay
        kv = lax.dot_general((k * state_decay).astype(BF16), v, (((0,), (0,)), ((), ())),
                             preferred_element_type=F32)
        state_ref[...] = state * chunk_decay + kv
        out = inner + cross
        mu = jnp.mean(out, axis=-1, keepdims=True)
        dev = out - mu
        var = jnp.mean(dev * dev, axis=-1, keepdims=True)
        g = g_ref[rows, :]
        o_ref[rows, :] = (dev * lax.rsqrt(var + EPS) * gn * (g * jax.nn.sigmoid(g))).astype(o_ref.dtype)


def _retention(proj, cos, sin, g_norm, bsz, seq, *, chunks_per_step=8):
    dh, h = RET_HEAD_DIM, RET_HEADS
    rb = chunks_per_step * RET_CHUNK
    steps = seq // rb
    mask, cd, sd, ch = _retention_tables()
    first = SSM_WIDTH // dh
    col = lambda which: pl.BlockSpec(
        (rb, dh), lambda b, hh, s, which=which: (b * steps + s, first + which * h + hh))
    rope = pl.BlockSpec((rb, dh // 2), lambda b, hh, s: (b * steps + s, 0))
    per_head = lambda rows: pl.BlockSpec((None, rows, dh), lambda b, hh, s: (hh, 0, 0))
    blocks = ([((rb, dh), F32)] * 4 + [((rb, dh // 2), F32)] * 2
              + [((RET_CHUNK, RET_CHUNK), F32), ((RET_CHUNK, dh), F32), ((RET_CHUNK, dh), F32),
                 ((1, dh), F32), ((1, dh), F32), ((rb, dh), BF16)])
    return pl.pallas_call(
        functools.partial(_retention_kernel, chunks_per_step=chunks_per_step),
        grid=(bsz, h, steps),
        in_specs=[col(0), col(1), col(2), col(3), rope, rope,
                  pl.BlockSpec((None, RET_CHUNK, RET_CHUNK), lambda b, hh, s: (hh, 0, 0)),
                  per_head(RET_CHUNK), per_head(RET_CHUNK), per_head(1),
                  pl.BlockSpec((1, dh), lambda b, hh, s: (0, hh))],
        out_specs=pl.BlockSpec((rb, dh), lambda b, hh, s: (b * steps + s, hh)),
        out_shape=jax.ShapeDtypeStruct((bsz * seq, RET_WIDTH), BF16),
        scratch_shapes=[pltpu.VMEM((dh, dh), F32)],
        compiler_params=_params(("parallel", "parallel", "arbitrary"), blocks, [((dh, dh), F32)]),
        name="retention",
    )(proj, proj, proj, proj, cos, sin, mask, cd, sd, ch, g_norm.reshape(1, RET_WIDTH))


def kernel(x, c, positions, w_ada, b_ada, g_mix, w_in, lam_re, lam_im, log_dt, b_re, b_im, c_re,
           c_im, d_skip, w_glu, b_glu, g_ssm_out, g_ret_norm, w_out, g_mlp, w_up, w_down, g_final):
    bsz, seq, d = x.shape
    depth = w_ada.shape[0]
    x2 = x.reshape(bsz * seq, d)
    c_pad = jnp.zeros((ADA_ROWS, d), F32).at[:bsz].set(c)
    b_ada3 = b_ada.reshape(depth, 1, b_ada.shape[1])
    mod = _ada_mod(c_pad, w_ada, b_ada3, 0)
    cos, sin = _rope_tables(positions)
    w_glu_b = w_glu.astype(BF16)
    for i in range(depth):
        shift1, scale1, gate1, shift2, scale2, gate2 = jnp.split(mod[:bsz], 6, axis=-1)

        h = _normmod(x2, g_mix[i], scale1, shift1, seq)
        proj, w_out_b = _matmul_cast(h, w_in, i, tm=1024, tn=1024, out_dtype=F32,
                                     side=w_out, side_rows=64, name="in_proj")
        tables = _s5_prep(lam_re[i], lam_im[i], log_dt[i], b_re[i], b_im[i], c_re[i], c_im[i])
        y = _s5_scan(proj, tables, d_skip[i], bsz, seq)
        ssm_out = _glu_norm(y, w_glu_b, i, b_glu[i], g_ssm_out[i])
        ret_out = _retention(proj, cos, sin, g_ret_norm[i], bsz, seq)
        x2 = _matmul([ssm_out, ret_out], w_out_b[None], 0, tm=1024, tn=1024, tk=d, out_dtype=F32,
                     epilogue=_epi_resid, resid=x2, gate=gate1, seq=seq, name="out_proj")

        h = _normmod(x2, g_mlp[i], scale2, shift2, seq)
        a, w_down_b = _matmul_cast(h, w_up, i, tm=1024, tn=1024, out_dtype=BF16,
                                   epilogue=_epi_relu2, side=w_down, side_rows=128, name="mlp_up")
        ada = (c_pad, w_ada, b_ada3, i + 1) if i + 1 < depth else None
        res = _matmul([a], w_down_b[None], 0, tm=512, tn=256, tk=w_down.shape[1], out_dtype=F32,
                      epilogue=_epi_resid, resid=x2, gate=gate2, seq=seq, ada=ada, name="mlp_down")
        x2, mod = res if ada is not None else (res, None)
    return _rmsnorm(x2, g_final).reshape(bsz, seq, d)
```

```python
import functools

import numpy as np
import jax
import jax.numpy as jnp
from jax import lax
from jax.experimental import pallas as pl
from jax.experimental.pallas import tpu as pltpu

F32 = jnp.float32
BF16 = jnp.bfloat16

SSM_WIDTH = 2048
RET_WIDTH = 2048
SSM_GROUP = 16
SSM_STATE = 64
RET_HEAD_DIM = 256
RET_HEADS = 8
RET_CHUNK = 128
ROPE_BASE = 10000.0
EPS = 1e-6

V7X_LANES = 128
V7X_VMEM_BYTES = 64 * 1024 * 1024
V7X_COMPILER_SCRATCH_BYTES = 6 * 1024 * 1024

S5_CHUNK = 256
S5_POW_BLOCK = 16
S5_GROUPS_PER_TILE = V7X_LANES // SSM_GROUP
S5_TILE_STATES = S5_GROUPS_PER_TILE * SSM_STATE


def _nbytes(shape, dtype):
    return int(np.prod(shape)) * jnp.dtype(dtype).itemsize


def _params(semantics, pipelined_blocks, scratch_blocks=()):
    need = 2 * sum(_nbytes(s, d) for s, d in pipelined_blocks)
    need += sum(_nbytes(s, d) for s, d in scratch_blocks)
    limit = min(need + V7X_COMPILER_SCRATCH_BYTES, V7X_VMEM_BYTES - 4 * 1024 * 1024)
    return pltpu.CompilerParams(dimension_semantics=semantics, vmem_limit_bytes=limit)


def _ada_kernel(c_ref, w_ref, b_ref, o_ref):
    c = c_ref[...]
    c_act = c * jax.nn.sigmoid(c)
    o_ref[...] = jnp.dot(c_act.astype(BF16), w_ref[...].astype(BF16),
                         preferred_element_type=F32) + b_ref[...]


ADA_ROWS = 8


def _ada_mod(c_pad, w_ada, b_ada3, layer, *, tn=1024):
    _, d, n = w_ada.shape
    rows = ADA_ROWS
    blocks = [((rows, d), F32), ((d, tn), F32), ((1, tn), F32), ((rows, tn), F32)]
    return pl.pallas_call(
        _ada_kernel,
        grid=(n // tn,),
        in_specs=[pl.BlockSpec((rows, d), lambda j: (0, 0)),
                  pl.BlockSpec((None, d, tn), lambda j: (layer, 0, j)),
                  pl.BlockSpec((None, 1, tn), lambda j: (layer, 0, j))],
        out_specs=pl.BlockSpec((rows, tn), lambda j: (0, j)),
        out_shape=jax.ShapeDtypeStruct((rows, n), F32),
        compiler_params=_params(("parallel",), blocks),
        name="ada_mod",
    )(c_pad, w_ada, b_ada3)


def _normmod_kernel(x_ref, g_ref, sc_ref, sh_ref, o_ref):
    x = x_ref[...]
    ms = jnp.mean(x * x, axis=-1, keepdims=True)
    y = x * lax.rsqrt(ms + EPS) * g_ref[...]
    o_ref[...] = (y * (1.0 + sc_ref[...]) + sh_ref[...]).astype(o_ref.dtype)


def _normmod(x2, g, scale, shift, seq, *, tm=512):
    t, d = x2.shape
    bsz = scale.shape[0]
    blocks = [((tm, d), F32), ((1, d), F32), ((1, d), F32), ((1, d), F32), ((tm, d), BF16)]
    per_b = pl.BlockSpec((None, 1, d), lambda i: (i * tm // seq, 0, 0))
    return pl.pallas_call(
        _normmod_kernel,
        grid=(t // tm,),
        in_specs=[pl.BlockSpec((tm, d), lambda i: (i, 0)),
                  pl.BlockSpec((1, d), lambda i: (0, 0)), per_b, per_b],
        out_specs=pl.BlockSpec((tm, d), lambda i: (i, 0)),
        out_shape=jax.ShapeDtypeStruct((t, d), BF16),
        compiler_params=_params(("parallel",), blocks, [((tm, d), F32)]),
        name="normmod",
    )(x2, g.reshape(1, d), scale.reshape(bsz, 1, d), shift.reshape(bsz, 1, d))


def _rmsnorm_kernel(x_ref, g_ref, o_ref):
    x = x_ref[...]
    ms = jnp.mean(x * x, axis=-1, keepdims=True)
    o_ref[...] = x * lax.rsqrt(ms + EPS) * g_ref[...]


def _rmsnorm(x2, g, *, tm=512):
    t, d = x2.shape
    blocks = [((tm, d), F32), ((1, d), F32), ((tm, d), F32)]
    return pl.pallas_call(
        _rmsnorm_kernel,
        grid=(t // tm,),
        in_specs=[pl.BlockSpec((tm, d), lambda i: (i, 0)),
                  pl.BlockSpec((1, d), lambda i: (0, 0))],
        out_specs=pl.BlockSpec((tm, d), lambda i: (i, 0)),
        out_shape=jax.ShapeDtypeStruct((t, d), F32),
        compiler_params=_params(("parallel",), blocks, [((tm, d), F32)]),
        name="final_rmsnorm",
    )(x2, g.reshape(1, d))


def _epi_none(acc):
    return acc


def _epi_relu2(acc):
    r = jnp.maximum(acc, 0.0)
    return r * r


def _epi_resid(acc, x_ref, gate_ref):
    return x_ref[...] + gate_ref[...] * acc


def _mm_kernel(*refs, n_a, nk, epilogue, has_ada):
    a_refs, w_ref, rest = refs[:n_a], refs[n_a], refs[n_a + 1:]
    if has_ada:
        n_main = len(rest) - (5 if nk == 1 else 6)
        c_ref, wa_ref, ba_ref = rest[n_main:n_main + 3]
        mod_ref = rest[n_main + 4]
        _ada_kernel(c_ref, wa_ref, ba_ref, mod_ref)
        rest = rest[:n_main] + (rest[n_main + 3],) + rest[n_main + 5:]

    def product():
        total, row = None, 0
        for a_ref in a_refs:
            kk = a_ref.shape[1]
            part = jnp.dot(a_ref[...], w_ref[row:row + kk, :], preferred_element_type=F32)
            total = part if total is None else total + part
            row += kk
        return total

    if nk == 1:
        *extra, o_ref = rest
        o_ref[...] = epilogue(product(), *extra).astype(o_ref.dtype)
        return
    *extra, o_ref, acc_ref = rest
    k = pl.program_id(2)

    @pl.when(k == 0)
    def _():
        acc_ref[...] = jnp.zeros_like(acc_ref)

    acc_ref[...] += product()

    @pl.when(k == nk - 1)
    def _():
        o_ref[...] = epilogue(acc_ref[...], *extra).astype(o_ref.dtype)


def _matmul(a_list, w, layer, *, tm, tn, tk, out_dtype, epilogue=_epi_none, resid=None,
            gate=None, seq=None, ada=None, name):
    m = a_list[0].shape[0]
    _, kdim, n = w.shape
    nk = kdim // tk
    if len(a_list) == 1:
        in_specs = [pl.BlockSpec((tm, tk), lambda i, j, k: (i, k))]
        blocks = [((tm, tk), a_list[0].dtype)]
    else:
        assert nk == 1 and sum(a.shape[1] for a in a_list) == kdim
        in_specs = [pl.BlockSpec((tm, a.shape[1]), lambda i, j, k: (i, 0)) for a in a_list]
        blocks = [((tm, a.shape[1]), a.dtype) for a in a_list]
    in_specs.append(pl.BlockSpec((None, tk, tn), lambda i, j, k: (layer, k, j)))
    blocks += [((tk, tn), w.dtype), ((tm, tn), out_dtype)]
    args = list(a_list) + [w]
    if resid is not None:
        bsz = gate.shape[0]
        in_specs += [pl.BlockSpec((tm, tn), lambda i, j, k: (i, j)),
                     pl.BlockSpec((None, 1, tn), lambda i, j, k: (i * tm // seq, 0, j))]
        blocks += [((tm, tn), F32), ((1, tn), F32)]
        args += [resid, gate.reshape(bsz, 1, n)]
    scratch = [] if nk == 1 else [pltpu.VMEM((tm, tn), F32)]
    scratch_blocks = [((tm, tn), F32)] * (1 if nk == 1 else 2)
    out_specs = pl.BlockSpec((tm, tn), lambda i, j, k: (i, j))
    out_shape = jax.ShapeDtypeStruct((m, n), out_dtype)
    if ada is not None:
        c_pad, w_ada, b_ada3, ada_layer = ada
        _, d_ada, n_ada = w_ada.shape
        n_blk = n_ada // V7X_LANES
        nj = n // tn
        assert n_blk <= (m // tm) * nj * nk
        blk = lambda i, j, k: jnp.minimum((i * nj + j) * nk + k, n_blk - 1)
        in_specs += [pl.BlockSpec((ADA_ROWS, d_ada), lambda i, j, k: (0, 0)),
                     pl.BlockSpec((None, d_ada, V7X_LANES), lambda i, j, k: (ada_layer, 0, blk(i, j, k))),
                     pl.BlockSpec((None, 1, V7X_LANES), lambda i, j, k: (ada_layer, 0, blk(i, j, k)))]
        out_specs = [out_specs, pl.BlockSpec((ADA_ROWS, V7X_LANES), lambda i, j, k: (0, blk(i, j, k)))]
        out_shape = [out_shape, jax.ShapeDtypeStruct((ADA_ROWS, n_ada), F32)]
        blocks += [((ADA_ROWS, d_ada), F32), ((d_ada, V7X_LANES), F32), ((1, V7X_LANES), F32),
                   ((ADA_ROWS, V7X_LANES), F32)]
        args += [c_pad, w_ada, b_ada3]
    return pl.pallas_call(
        functools.partial(_mm_kernel, n_a=len(a_list), nk=nk, epilogue=epilogue,
                          has_ada=ada is not None),
        grid=(m // tm, n // tn, nk),
        in_specs=in_specs,
        out_specs=out_specs,
        out_shape=out_shape,
        scratch_shapes=scratch,
        compiler_params=_params(("arbitrary",) * 3 if ada is not None
                                else ("parallel", "parallel", "arbitrary"), blocks, scratch_blocks),
        name=name,
    )(*args)


def _mm_cast_kernel(a_ref, wc_ref, *rest, epilogue, has_side):
    if has_side:
        side_ref, o_ref, side_o_ref, wb_ref = rest
        side_o_ref[...] = side_ref[...].astype(BF16)
    else:
        o_ref, wb_ref = rest
    j = pl.program_id(0)
    i = pl.program_id(1)
    nj = pl.num_programs(0) - 1
    rows = wc_ref.shape[0]

    @pl.when(j < nj)
    def _():
        wb_ref[j % 2, pl.ds(pl.multiple_of(i * rows, rows), rows), :] = wc_ref[...].astype(BF16)

    @pl.when(j > 0)
    def _():
        prod = jnp.dot(a_ref[...], wb_ref[(j - 1) % 2], preferred_element_type=F32)
        o_ref[...] = epilogue(prod).astype(o_ref.dtype)


def _matmul_cast(a, w, layer, *, tm, tn, out_dtype, epilogue=_epi_none, side=None,
                 side_rows=None, name):
    m, kdim = a.shape
    n = w.shape[2]
    ni, nj = m // tm, n // tn
    rows = kdim // ni
    row_blk = lambda j, i: jnp.where(j > 0, i, 0)
    col_blk = lambda j: jnp.maximum(j - 1, 0)
    blocks = [((tm, kdim), a.dtype), ((rows, tn), w.dtype), ((tm, tn), out_dtype)]
    scratch_blocks = [((2, kdim, tn), BF16), ((tm, tn), F32)]
    in_specs = [pl.BlockSpec((tm, kdim), lambda j, i: (row_blk(j, i), 0)),
                pl.BlockSpec((None, rows, tn),
                             lambda j, i: (layer, jnp.where(j < nj, i, ni - 1),
                                           jnp.minimum(j, nj - 1)))]
    out_specs = pl.BlockSpec((tm, tn), lambda j, i: (row_blk(j, i), col_blk(j)))
    out_shape = jax.ShapeDtypeStruct((m, n), out_dtype)
    args = [a, w]
    if side is not None:
        _, r, c = side.shape
        n_side = r // side_rows
        assert n_side * side_rows == r and n_side <= (nj + 1) * ni
        side_blk = lambda j, i: jnp.minimum(j * ni + i, n_side - 1)
        in_specs.append(pl.BlockSpec((None, side_rows, c), lambda j, i: (layer, side_blk(j, i), 0)))
        out_specs = [out_specs, pl.BlockSpec((side_rows, c), lambda j, i: (side_blk(j, i), 0))]
        out_shape = [out_shape, jax.ShapeDtypeStruct((r, c), BF16)]
        blocks += [((side_rows, c), F32), ((side_rows, c), BF16)]
        args.append(side)
    return pl.pallas_call(
        functools.partial(_mm_cast_kernel, epilogue=epilogue, has_side=side is not None),
        grid=(nj + 1, ni),
        in_specs=in_specs,
        out_specs=out_specs,
        out_shape=out_shape,
        scratch_shapes=[pltpu.VMEM((2, kdim, tn), BF16)],
        compiler_params=_params(("arbitrary", "arbitrary"), blocks, scratch_blocks),
        name=name,
    )(*args)


def _rope_kernel(pos_ref, inv_ref, cos_ref, sin_ref):
    ang = pos_ref[...].astype(F32) * inv_ref[...]
    cos_ref[...] = jnp.cos(ang)
    sin_ref[...] = jnp.sin(ang)


def _rope_tables(positions, *, tm=1024):
    t = positions.size
    half = RET_HEAD_DIM // 2
    inv = ROPE_BASE ** (-np.arange(0, RET_HEAD_DIM, 2, dtype=np.float64) / RET_HEAD_DIM)
    inv = jnp.asarray(inv.astype(np.float32)).reshape(1, half)
    blocks = [((tm, 1), jnp.int32), ((1, half), F32), ((tm, half), F32), ((tm, half), F32)]
    return pl.pallas_call(
        _rope_kernel,
        grid=(t // tm,),
        in_specs=[pl.BlockSpec((tm, 1), lambda i: (i, 0)),
                  pl.BlockSpec((1, half), lambda i: (0, 0))],
        out_specs=[pl.BlockSpec((tm, half), lambda i: (i, 0))] * 2,
        out_shape=[jax.ShapeDtypeStruct((t, half), F32)] * 2,
        compiler_params=_params(("parallel",), blocks),
        name="rope_tables",
    )(positions.reshape(t, 1), inv)


def _s5_prep_kernel(lr_ref, li_ref, ldt_ref, br_ref, bi_ref, cr_ref, ci_ref,
                    bblk_ref, cblk_ref, pn_ref, pp_ref):
    w = S5_TILE_STATES
    lr = lr_ref[...]
    li = li_ref[...]
    dt = jnp.exp(ldt_ref[...])
    a = lr * dt
    th = li * dt
    mag = jnp.exp(a)
    lbr = mag * jnp.cos(th)
    lbi = mag * jnp.sin(th)
    den = lr * lr + li * li
    nr = lbr - 1.0
    coef_r = (nr * lr + lbi * li) / den
    coef_i = (lbi * lr - nr * li) / den
    br = br_ref[...]
    bi = bi_ref[...]
    bbar_r = coef_r * br - coef_i * bi
    bbar_i = coef_r * bi + coef_i * br

    rows = lax.broadcasted_iota(jnp.int32, (V7X_LANES, w), 0)
    cols = lax.broadcasted_iota(jnp.int32, (V7X_LANES, w), 1)
    same_group = ((rows >> (SSM_GROUP.bit_length() - 1))
                  == (cols >> (SSM_STATE.bit_length() - 1)))
    reps = (S5_GROUPS_PER_TILE, 1)

    def blockdiag(v):
        return jnp.where(same_group, jnp.tile(v, reps), 0.0)

    bblk_ref[:, :w] = blockdiag(bbar_r).astype(bblk_ref.dtype)
    bblk_ref[:, w:] = blockdiag(bbar_i).astype(bblk_ref.dtype)
    cblk_ref[:, :w] = blockdiag(cr_ref[...]).astype(cblk_ref.dtype)
    cblk_ref[:, w:] = blockdiag(-ci_ref[...]).astype(cblk_ref.dtype)

    blk = S5_POW_BLOCK
    lo_m = (lax.broadcasted_iota(jnp.int32, (blk, w), 0) + 1).astype(F32)
    hi_m = (lax.broadcasted_iota(jnp.int32, (S5_CHUNK // blk, w), 0) * blk).astype(F32)

    def powers(m, sign):
        mag_m = jnp.exp(sign * (m * a))
        return mag_m * jnp.cos(m * th), sign * (mag_m * jnp.sin(m * th))

    for sign, ref in ((1.0, pp_ref), (-1.0, pn_ref)):
        lo_r, lo_i = powers(lo_m, sign)
        hi_r, hi_i = powers(hi_m, sign)
        for q in range(S5_CHUNK // blk):
            h_re, h_im = hi_r[q:q + 1, :], hi_i[q:q + 1, :]
            ref[q * blk:(q + 1) * blk, :w] = h_re * lo_r - h_im * lo_i
            ref[q * blk:(q + 1) * blk, w:] = h_re * lo_i + h_im * lo_r


def _s5_prep(lam_re, lam_im, log_dt, b_re, b_im, c_re, c_im):
    g, p, h = b_re.shape
    gl, w = S5_GROUPS_PER_TILE, S5_TILE_STATES
    nt = g // gl
    row = lambda v: v.reshape(nt, 1, w)
    bt = lambda v: v.reshape(nt, gl, p, h).transpose(0, 3, 1, 2).reshape(nt, h, w)
    ct = lambda v: v.reshape(nt, gl, h, p).transpose(0, 2, 1, 3).reshape(nt, h, w)
    row_spec = pl.BlockSpec((None, 1, w), lambda j: (j, 0, 0))
    mat_spec = pl.BlockSpec((None, h, w), lambda j: (j, 0, 0))
    blk_spec = pl.BlockSpec((None, V7X_LANES, 2 * w), lambda j: (j, 0, 0))
    pow_spec = pl.BlockSpec((None, S5_CHUNK, 2 * w), lambda j: (j, 0, 0))
    blocks = ([((1, w), F32)] * 3 + [((h, w), F32)] * 4 + [((V7X_LANES, 2 * w), BF16)] * 2
              + [((S5_CHUNK, 2 * w), F32)] * 2)
    return pl.pallas_call(
        _s5_prep_kernel,
        grid=(nt,),
        in_specs=[row_spec] * 3 + [mat_spec] * 4,
        out_specs=[blk_spec, blk_spec, pow_spec, pow_spec],
        out_shape=[jax.ShapeDtypeStruct((nt, V7X_LANES, 2 * w), BF16)] * 2
        + [jax.ShapeDtypeStruct((nt, S5_CHUNK, 2 * w), F32)] * 2,
        compiler_params=_params(("parallel",), blocks),
        name="s5_prep",
    )(row(lam_re), row(lam_im), row(jnp.repeat(log_dt, p)), bt(b_re), bt(b_im), ct(c_re), ct(c_im))


def _s5_kernel(u_ref, bblk_ref, cblk_ref, pn_ref, pp_ref, d_ref, y_ref, *, n_chunks):
    w = S5_TILE_STATES
    lc = S5_CHUNK
    bblk = bblk_ref[...]
    cblk = cblk_ref[...]
    d = d_ref[...]
    r = lax.broadcasted_iota(jnp.int32, (lc, lc), 0)
    c = lax.broadcasted_iota(jnp.int32, (lc, lc), 1)
    tri = (r >= c).astype(BF16)

    def chunk(t, carry):
        cr, ci = carry
        rows = pl.ds(pl.multiple_of(t * lc, lc), lc)
        u = u_ref[rows, :]
        bu = jnp.dot(u.astype(BF16), bblk, preferred_element_type=F32)
        br, bi = bu[:, :w], bu[:, w:]
        pnr, pni = pn_ref[:, :w], pn_ref[:, w:]
        z = jnp.concatenate([pnr * br - pni * bi, pnr * bi + pni * br], axis=1)
        ws = jnp.dot(tri, z.astype(BF16), preferred_element_type=F32)
        wr = ws[:, :w] + cr
        wi = ws[:, w:] + ci
        ppr, ppi = pp_ref[:, :w], pp_ref[:, w:]
        xr = ppr * wr - ppi * wi
        xi = ppr * wi + ppi * wr
        x = jnp.concatenate([xr, xi], axis=1).astype(BF16)
        y = lax.dot_general(x, cblk, (((1,), (1,)), ((), ())), preferred_element_type=F32)
        y_ref[rows, :] = jax.nn.gelu(y + d * u)
        return xr[lc - 1:, :], xi[lc - 1:, :]

    zero = jnp.zeros((1, w), F32)
    lax.fori_loop(0, n_chunks, chunk, (zero, zero), unroll=8)


def _s5_scan(proj, tables, d_skip, bsz, seq):
    bblk, cblk, pn, pp = tables
    nt = bblk.shape[0]
    w2 = 2 * S5_TILE_STATES
    tab = lambda rows: pl.BlockSpec((None, rows, w2), lambda j, b: (j, 0, 0))
    blocks = [((seq, V7X_LANES), F32), ((V7X_LANES, w2), BF16), ((V7X_LANES, w2), BF16),
              ((S5_CHUNK, w2), F32), ((S5_CHUNK, w2), F32), ((1, V7X_LANES), F32),
              ((seq, V7X_LANES), F32)]
    return pl.pallas_call(
        functools.partial(_s5_kernel, n_chunks=seq // S5_CHUNK),
        grid=(nt, bsz),
        in_specs=[pl.BlockSpec((seq, V7X_LANES), lambda j, b: (b, j)),
                  tab(V7X_LANES), tab(V7X_LANES), tab(S5_CHUNK), tab(S5_CHUNK),
                  pl.BlockSpec((1, V7X_LANES), lambda j, b: (0, j))],
        out_specs=pl.BlockSpec((seq, V7X_LANES), lambda j, b: (b, j)),
        out_shape=jax.ShapeDtypeStruct((bsz * seq, SSM_WIDTH), F32),
        compiler_params=_params(("parallel", "parallel"), blocks),
        name="s5_scan",
    )(proj, bblk, cblk, pn, pp, d_skip.reshape(1, SSM_WIDTH))


def _glu_kernel(y_ref, w_ref, b_ref, g_ref, o_ref):
    y = y_ref[...]
    z = jnp.dot(y.astype(BF16), w_ref[...], preferred_element_type=F32) + b_ref[...]
    v = y * jax.nn.sigmoid(z)
    ms = jnp.mean(v * v, axis=-1, keepdims=True)
    o_ref[...] = (v * lax.rsqrt(ms + EPS) * g_ref[...]).astype(o_ref.dtype)


def _glu_norm(y, w_glu, layer, b_glu, g_out, *, tm=512):
    t, n = y.shape
    blocks = [((tm, n), F32), ((n, n), BF16), ((1, n), F32), ((1, n), F32), ((tm, n), BF16)]
    vec = pl.BlockSpec((1, n), lambda i: (0, 0))
    return pl.pallas_call(
        _glu_kernel,
        grid=(t // tm,),
        in_specs=[pl.BlockSpec((tm, n), lambda i: (i, 0)),
                  pl.BlockSpec((None, n, n), lambda i: (layer, 0, 0)), vec, vec],
        out_specs=pl.BlockSpec((tm, n), lambda i: (i, 0)),
        out_shape=jax.ShapeDtypeStruct((t, n), BF16),
        compiler_params=_params(("parallel",), blocks),
        name="glu_norm",
    )(y, w_glu, b_glu.reshape(1, n), g_out.reshape(1, n))


def _retention_tables():
    h, c, dh = RET_HEADS, RET_CHUNK, RET_HEAD_DIM
    log_gamma = np.log1p(-np.exp2(-5.0 - np.arange(h, dtype=np.float64)))
    idx = np.arange(c, dtype=np.float64)
    rel = idx[:, None] - idx[None, :]
    mask = np.where(rel >= 0, np.exp(log_gamma[:, None, None] * np.maximum(rel, 0.0)), 0.0)
    cross = np.exp(log_gamma[:, None] * (idx + 1.0))
    state = np.exp(log_gamma[:, None] * (c - 1.0 - idx))
    chunk = np.exp(log_gamma * c)
    k_scale = dh ** -0.5
    assert np.log2(k_scale) == np.round(np.log2(k_scale))
    mask = mask * k_scale
    state = state * k_scale
    wide = lambda v: np.broadcast_to(v[:, :, None], (h, c, dh))
    f = lambda v: jnp.asarray(np.ascontiguousarray(v).astype(np.float32))
    return f(mask), f(wide(cross)), f(wide(state)), f(np.broadcast_to(chunk[:, None, None], (h, 1, dh)))


def _retention_kernel(q_ref, k_ref, v_ref, g_ref, cos_ref, sin_ref, mask_ref, cd_ref, sd_ref,
                      ch_ref, gn_ref, o_ref, state_ref, *, chunks_per_step):
    half = RET_HEAD_DIM // 2
    c = RET_CHUNK

    @pl.when(pl.program_id(2) == 0)
    def _():
        state_ref[...] = jnp.zeros_like(state_ref)

    mask = mask_ref[...]
    cross_decay = cd_ref[...]
    state_decay = sd_ref[...]
    chunk_decay = ch_ref[...]
    gn = gn_ref[...]

    def rotate(t, cos, sin):
        t1, t2 = t[:, :half], t[:, half:]
        return jnp.concatenate([t1 * cos - t2 * sin, t1 * sin + t2 * cos], axis=1)

    for ci in range(chunks_per_step):
        rows = pl.ds(ci * c, c)
        cos = cos_ref[rows, :]
        sin = sin_ref[rows, :]
        q = rotate(q_ref[rows, :], cos, sin)
        k = rotate(k_ref[rows, :], cos, sin)
        v = v_ref[rows, :].astype(BF16)
        qb = q.astype(BF16)
        state = state_ref[...]
        scores = lax.dot_general(qb, k.astype(BF16), (((1,), (1,)), ((), ())),
                                 preferred_element_type=F32) * mask
        inner = jnp.dot(scores.astype(BF16), v, preferred_element_type=F32)
        cross = jnp.dot(qb, state.astype(BF16), preferred_element_type=F32) * cross_decay
        kv = lax.dot_general((k * state_decay).astype(BF16), v, (((0,), (0,)), ((), ())),
                             preferred_element_type=F32)
        state_ref[...] = state * chunk_decay + kv
        out = inner + cross
        mu = jnp.mean(out, axis=-1, keepdims=True)
        dev = out - mu
        var = jnp.mean(dev * dev, axis=-1, keepdims=True)
        g = g_ref[rows, :]
        o_ref[rows, :] = (dev * lax.rsqrt(var + EPS) * gn * (g * jax.nn.sigmoid(g))).astype(o_ref.dtype)


def _retention(proj, cos, sin, g_norm, bsz, seq, *, chunks_per_step=8):
    dh, h = RET_HEAD_DIM, RET_HEADS
    rb = chunks_per_step * RET_CHUNK
    steps = seq // rb
    mask, cd, sd, ch = _retention_tables()
    first = SSM_WIDTH // dh
    col = lambda which: pl.BlockSpec(
        (rb, dh), lambda b, hh, s, which=which: (b * steps + s, first + which * h + hh))
    rope = pl.BlockSpec((rb, dh // 2), lambda b, hh, s: (b * steps + s, 0))
    per_head = lambda rows: pl.BlockSpec((None, rows, dh), lambda b, hh, s: (hh, 0, 0))
    blocks = ([((rb, dh), F32)] * 4 + [((rb, dh // 2), F32)] * 2
              + [((RET_CHUNK, RET_CHUNK), F32), ((RET_CHUNK, dh), F32), ((RET_CHUNK, dh), F32),
                 ((1, dh), F32), ((1, dh), F32), ((rb, dh), BF16)])
    return pl.pallas_call(
        functools.partial(_retention_kernel, chunks_per_step=chunks_per_step),
        grid=(bsz, h, steps),
        in_specs=[col(0), col(1), col(2), col(3), rope, rope,
                  pl.BlockSpec((None, RET_CHUNK, RET_CHUNK), lambda b, hh, s: (hh, 0, 0)),
                  per_head(RET_CHUNK), per_head(RET_CHUNK), per_head(1),
                  pl.BlockSpec((1, dh), lambda b, hh, s: (0, hh))],
        out_specs=pl.BlockSpec((rb, dh), lambda b, hh, s: (b * steps + s, hh)),
        out_shape=jax.ShapeDtypeStruct((bsz * seq, RET_WIDTH), BF16),
        scratch_shapes=[pltpu.VMEM((dh, dh), F32)],
        compiler_params=_params(("parallel", "parallel", "arbitrary"), blocks, [((dh, dh), F32)]),
        name="retention",
    )(proj, proj, proj, proj, cos, sin, mask, cd, sd, ch, g_norm.reshape(1, RET_WIDTH))


def kernel(x, c, positions, w_ada, b_ada, g_mix, w_in, lam_re, lam_im, log_dt, b_re, b_im, c_re,
           c_im, d_skip, w_glu, b_glu, g_ssm_out, g_ret_norm, w_out, g_mlp, w_up, w_down, g_final):
    bsz, seq, d = x.shape
    depth = w_ada.shape[0]
    x2 = x.reshape(bsz * seq, d)
    c_pad = jnp.zeros((ADA_ROWS, d), F32).at[:bsz].set(c)
    b_ada3 = b_ada.reshape(depth, 1, b_ada.shape[1])
    mod = _ada_mod(c_pad, w_ada, b_ada3, 0)
    cos, sin = _rope_tables(positions)
    w_glu_b = w_glu.astype(BF16)
    for i in range(depth):
        shift1, scale1, gate1, shift2, scale2, gate2 = jnp.split(mod[:bsz], 6, axis=-1)

        h = _normmod(x2, g_mix[i], scale1, shift1, seq)
        proj, w_out_b = _matmul_cast(h, w_in, i, tm=1024, tn=1024, out_dtype=F32,
                                     side=w_out, side_rows=64, name="in_proj")
        tables = _s5_prep(lam_re[i], lam_im[i], log_dt[i], b_re[i], b_im[i], c_re[i], c_im[i])
        y = _s5_scan(proj, tables, d_skip[i], bsz, seq)
        ssm_out = _glu_norm(y, w_glu_b, i, b_glu[i], g_ssm_out[i])
        ret_out = _retention(proj, cos, sin, g_ret_norm[i], bsz, seq)
        x2 = _matmul([ssm_out, ret_out], w_out_b[None], 0, tm=1024, tn=1024, tk=d, out_dtype=F32,
                     epilogue=_epi_resid, resid=x2, gate=gate1, seq=seq, name="out_proj")

        h = _normmod(x2, g_mlp[i], scale2, shift2, seq)
        a, w_down_b = _matmul_cast(h, w_up, i, tm=1024, tn=1024, out_dtype=BF16,
                                   epilogue=_epi_relu2, side=w_down, side_rows=128, name="mlp_up")
        ada = (c_pad, w_ada, b_ada3, i + 1) if i + 1 < depth else None
        res = _matmul([a], w_down_b[None], 0, tm=1024, tn=1024, tk=2048, out_dtype=F32,
                      epilogue=_epi_resid, resid=x2, gate=gate2, seq=seq, ada=ada, name="mlp_down")
        x2, mod = res if ada is not None else (res, None)
    return _rmsnorm(x2, g_final).reshape(bsz, seq, d)
```

```python
import functools

import numpy as np
import jax
import jax.numpy as jnp
from jax import lax
from jax.experimental import pallas as pl
from jax.experimental.pallas import tpu as pltpu

F32 = jnp.float32
BF16 = jnp.bfloat16

SSM_WIDTH = 2048
RET_WIDTH = 2048
SSM_GROUP = 16
SSM_STATE = 64
RET_HEAD_DIM = 256
RET_HEADS = 8
RET_CHUNK = 128
ROPE_BASE = 10000.0
EPS = 1e-6

V7X_LANES = 128
V7X_VMEM_BYTES = 64 * 1024 * 1024
V7X_COMPILER_SCRATCH_BYTES = 6 * 1024 * 1024

S5_CHUNK = 256
S5_POW_BLOCK = 16
S5_GROUPS_PER_TILE = V7X_LANES // SSM_GROUP
S5_TILE_STATES = S5_GROUPS_PER_TILE * SSM_STATE


def _nbytes(shape, dtype):
    return int(np.prod(shape)) * jnp.dtype(dtype).itemsize


def _params(semantics, pipelined_blocks, scratch_blocks=()):
    need = 2 * sum(_nbytes(s, d) for s, d in pipelined_blocks)
    need += sum(_nbytes(s, d) for s, d in scratch_blocks)
    limit = min(need + V7X_COMPILER_SCRATCH_BYTES, V7X_VMEM_BYTES - 4 * 1024 * 1024)
    return pltpu.CompilerParams(dimension_semantics=semantics, vmem_limit_bytes=limit)


def _ada_kernel(c_ref, w_ref, b_ref, o_ref):
    c = c_ref[...]
    c_act = c * jax.nn.sigmoid(c)
    o_ref[...] = jnp.dot(c_act.astype(BF16), w_ref[...].astype(BF16),
                         preferred_element_type=F32) + b_ref[...]


ADA_ROWS = 8
ADA_SIDE_K = 512
ADA_SIDE_N = 1024


def _ada_side(c_ref, w_ref, b_ref, mod_ref, step):
    kc, nc = w_ref.shape
    n_kb = c_ref.shape[1] // kc
    n_tiles = n_kb * (mod_ref.shape[1] // nc)

    @pl.when(step < n_tiles)
    def _():
        kb = step % n_kb
        c = c_ref[:, pl.ds(pl.multiple_of(kb * kc, kc), kc)]
        c_act = c * jax.nn.sigmoid(c)
        part = jnp.dot(c_act.astype(BF16), w_ref[...].astype(BF16), preferred_element_type=F32)
        cols = pl.ds(pl.multiple_of((step // n_kb) * nc, nc), nc)

        @pl.when(kb == 0)
        def _():
            mod_ref[:, cols] = part + b_ref[:, cols]

        @pl.when(kb > 0)
        def _():
            mod_ref[:, cols] += part


def _ada_mod(c_pad, w_ada, b_ada3, layer, *, tn=1024):
    _, d, n = w_ada.shape
    rows = ADA_ROWS
    blocks = [((rows, d), F32), ((d, tn), F32), ((1, tn), F32), ((rows, tn), F32)]
    return pl.pallas_call(
        _ada_kernel,
        grid=(n // tn,),
        in_specs=[pl.BlockSpec((rows, d), lambda j: (0, 0)),
                  pl.BlockSpec((None, d, tn), lambda j: (layer, 0, j)),
                  pl.BlockSpec((None, 1, tn), lambda j: (layer, 0, j))],
        out_specs=pl.BlockSpec((rows, tn), lambda j: (0, j)),
        out_shape=jax.ShapeDtypeStruct((rows, n), F32),
        compiler_params=_params(("parallel",), blocks),
        name="ada_mod",
    )(c_pad, w_ada, b_ada3)


def _normmod_kernel(x_ref, g_ref, sc_ref, sh_ref, o_ref):
    x = x_ref[...]
    ms = jnp.mean(x * x, axis=-1, keepdims=True)
    y = x * lax.rsqrt(ms + EPS) * g_ref[...]
    o_ref[...] = (y * (1.0 + sc_ref[...]) + sh_ref[...]).astype(o_ref.dtype)


def _normmod(x2, g, scale, shift, seq, *, tm=512):
    t, d = x2.shape
    bsz = scale.shape[0]
    blocks = [((tm, d), F32), ((1, d), F32), ((1, d), F32), ((1, d), F32), ((tm, d), BF16)]
    per_b = pl.BlockSpec((None, 1, d), lambda i: (i * tm // seq, 0, 0))
    return pl.pallas_call(
        _normmod_kernel,
        grid=(t // tm,),
        in_specs=[pl.BlockSpec((tm, d), lambda i: (i, 0)),
                  pl.BlockSpec((1, d), lambda i: (0, 0)), per_b, per_b],
        out_specs=pl.BlockSpec((tm, d), lambda i: (i, 0)),
        out_shape=jax.ShapeDtypeStruct((t, d), BF16),
        compiler_params=_params(("parallel",), blocks, [((tm, d), F32)]),
        name="normmod",
    )(x2, g.reshape(1, d), scale.reshape(bsz, 1, d), shift.reshape(bsz, 1, d))


def _rmsnorm_kernel(x_ref, g_ref, o_ref):
    x = x_ref[...]
    ms = jnp.mean(x * x, axis=-1, keepdims=True)
    o_ref[...] = x * lax.rsqrt(ms + EPS) * g_ref[...]


def _rmsnorm(x2, g, *, tm=512):
    t, d = x2.shape
    blocks = [((tm, d), F32), ((1, d), F32), ((tm, d), F32)]
    return pl.pallas_call(
        _rmsnorm_kernel,
        grid=(t // tm,),
        in_specs=[pl.BlockSpec((tm, d), lambda i: (i, 0)),
                  pl.BlockSpec((1, d), lambda i: (0, 0))],
        out_specs=pl.BlockSpec((tm, d), lambda i: (i, 0)),
        out_shape=jax.ShapeDtypeStruct((t, d), F32),
        compiler_params=_params(("parallel",), blocks, [((tm, d), F32)]),
        name="final_rmsnorm",
    )(x2, g.reshape(1, d))


def _epi_none(acc):
    return acc


def _epi_relu2(acc):
    r = jnp.maximum(acc, 0.0)
    return r * r


def _epi_resid(acc, x_ref, gate_ref):
    return x_ref[...] + gate_ref[...] * acc


def _mm_kernel(*refs, n_a, nk, epilogue, has_ada):
    a_refs, w_ref, rest = refs[:n_a], refs[n_a], refs[n_a + 1:]
    if has_ada:
        n_main = len(rest) - (5 if nk == 1 else 6)
        c_ref, wa_ref, ba_ref = rest[n_main:n_main + 3]
        mod_ref = rest[n_main + 4]
        step = ((pl.program_id(0) * pl.num_programs(1) + pl.program_id(1)) * nk
                + pl.program_id(2))
        _ada_side(c_ref, wa_ref, ba_ref, mod_ref, step)
        rest = rest[:n_main] + (rest[n_main + 3],) + rest[n_main + 5:]

    def product():
        total, row = None, 0
        for a_ref in a_refs:
            kk = a_ref.shape[1]
            part = jnp.dot(a_ref[...], w_ref[row:row + kk, :], preferred_element_type=F32)
            total = part if total is None else total + part
            row += kk
        return total

    if nk == 1:
        *extra, o_ref = rest
        o_ref[...] = epilogue(product(), *extra).astype(o_ref.dtype)
        return
    *extra, o_ref, acc_ref = rest
    k = pl.program_id(2)

    @pl.when(k == 0)
    def _():
        acc_ref[...] = jnp.zeros_like(acc_ref)

    acc_ref[...] += product()

    @pl.when(k == nk - 1)
    def _():
        o_ref[...] = epilogue(acc_ref[...], *extra).astype(o_ref.dtype)


def _matmul(a_list, w, layer, *, tm, tn, tk, out_dtype, epilogue=_epi_none, resid=None,
            gate=None, seq=None, ada=None, name):
    m = a_list[0].shape[0]
    _, kdim, n = w.shape
    nk = kdim // tk
    if len(a_list) == 1:
        in_specs = [pl.BlockSpec((tm, tk), lambda i, j, k: (i, k))]
        blocks = [((tm, tk), a_list[0].dtype)]
    else:
        assert nk == 1 and sum(a.shape[1] for a in a_list) == kdim
        in_specs = [pl.BlockSpec((tm, a.shape[1]), lambda i, j, k: (i, 0)) for a in a_list]
        blocks = [((tm, a.shape[1]), a.dtype) for a in a_list]
    in_specs.append(pl.BlockSpec((None, tk, tn), lambda i, j, k: (layer, k, j)))
    blocks += [((tk, tn), w.dtype), ((tm, tn), out_dtype)]
    args = list(a_list) + [w]
    if resid is not None:
        bsz = gate.shape[0]
        in_specs += [pl.BlockSpec((tm, tn), lambda i, j, k: (i, j)),
                     pl.BlockSpec((None, 1, tn), lambda i, j, k: (i * tm // seq, 0, j))]
        blocks += [((tm, tn), F32), ((1, tn), F32)]
        args += [resid, gate.reshape(bsz, 1, n)]
    scratch = [] if nk == 1 else [pltpu.VMEM((tm, tn), F32)]
    scratch_blocks = [((tm, tn), F32)] * (1 if nk == 1 else 2)
    out_specs = pl.BlockSpec((tm, tn), lambda i, j, k: (i, j))
    out_shape = jax.ShapeDtypeStruct((m, n), out_dtype)
    if ada is not None:
        c_pad, w_ada, b_ada3, ada_layer = ada
        _, d_ada, n_ada = w_ada.shape
        n_kb, n_cb = d_ada // ADA_SIDE_K, n_ada // ADA_SIDE_N
        nj = n // tn
        assert n_kb * n_cb <= (m // tm) * nj * nk
        tile = lambda i, j, k: jnp.minimum((i * nj + j) * nk + k, n_kb * n_cb - 1)
        in_specs += [pl.BlockSpec((ADA_ROWS, d_ada), lambda i, j, k: (0, 0)),
                     pl.BlockSpec((None, ADA_SIDE_K, ADA_SIDE_N),
                                  lambda i, j, k: (ada_layer, tile(i, j, k) % n_kb, tile(i, j, k) // n_kb)),
                     pl.BlockSpec((None, 1, n_ada), lambda i, j, k: (ada_layer, 0, 0))]
        out_specs = [out_specs, pl.BlockSpec((ADA_ROWS, n_ada), lambda i, j, k: (0, 0))]
        out_shape = [out_shape, jax.ShapeDtypeStruct((ADA_ROWS, n_ada), F32)]
        blocks += [((ADA_ROWS, d_ada), F32), ((ADA_SIDE_K, ADA_SIDE_N), F32), ((1, n_ada), F32),
                   ((ADA_ROWS, n_ada), F32)]
        args += [c_pad, w_ada, b_ada3]
    return pl.pallas_call(
        functools.partial(_mm_kernel, n_a=len(a_list), nk=nk, epilogue=epilogue,
                          has_ada=ada is not None),
        grid=(m // tm, n // tn, nk),
        in_specs=in_specs,
        out_specs=out_specs,
        out_shape=out_shape,
        scratch_shapes=scratch,
        compiler_params=_params(("arbitrary",) * 3 if ada is not None
                                else ("parallel", "parallel", "arbitrary"), blocks, scratch_blocks),
        name=name,
    )(*args)


def _mm_cast_kernel(a_ref, wc_ref, *rest, epilogue, has_side):
    if has_side:
        side_ref, o_ref, side_o_ref, wb_ref = rest
        side_o_ref[...] = side_ref[...].astype(BF16)
    else:
        o_ref, wb_ref = rest
    j = pl.program_id(0)
    i = pl.program_id(1)
    nj = pl.num_programs(0) - 1
    rows = wc_ref.shape[0]

    @pl.when(j < nj)
    def _():
        wb_ref[j % 2, pl.ds(pl.multiple_of(i * rows, rows), rows), :] = wc_ref[...].astype(BF16)

    @pl.when(j > 0)
    def _():
        prod = jnp.dot(a_ref[...], wb_ref[(j - 1) % 2], preferred_element_type=F32)
        o_ref[...] = epilogue(prod).astype(o_ref.dtype)


def _matmul_cast(a, w, layer, *, tm, tn, out_dtype, epilogue=_epi_none, side=None,
                 side_rows=None, name):
    m, kdim = a.shape
    n = w.shape[2]
    ni, nj = m // tm, n // tn
    rows = kdim // ni
    row_blk = lambda j, i: jnp.where(j > 0, i, 0)
    col_blk = lambda j: jnp.maximum(j - 1, 0)
    blocks = [((tm, kdim), a.dtype), ((rows, tn), w.dtype), ((tm, tn), out_dtype)]
    scratch_blocks = [((2, kdim, tn), BF16), ((tm, tn), F32)]
    in_specs = [pl.BlockSpec((tm, kdim), lambda j, i: (row_blk(j, i), 0)),
                pl.BlockSpec((None, rows, tn),
                             lambda j, i: (layer, jnp.where(j < nj, i, ni - 1),
                                           jnp.minimum(j, nj - 1)))]
    out_specs = pl.BlockSpec((tm, tn), lambda j, i: (row_blk(j, i), col_blk(j)))
    out_shape = jax.ShapeDtypeStruct((m, n), out_dtype)
    args = [a, w]
    if side is not None:
        _, r, c = side.shape
        n_side = r // side_rows
        assert n_side * side_rows == r and n_side <= (nj + 1) * ni
        side_blk = lambda j, i: jnp.minimum(j * ni + i, n_side - 1)
        in_specs.append(pl.BlockSpec((None, side_rows, c), lambda j, i: (layer, side_blk(j, i), 0)))
        out_specs = [out_specs, pl.BlockSpec((side_rows, c), lambda j, i: (side_blk(j, i), 0))]
        out_shape = [out_shape, jax.ShapeDtypeStruct((r, c), BF16)]
        blocks += [((side_rows, c), F32), ((side_rows, c), BF16)]
        args.append(side)
    return pl.pallas_call(
        functools.partial(_mm_cast_kernel, epilogue=epilogue, has_side=side is not None),
        grid=(nj + 1, ni),
        in_specs=in_specs,
        out_specs=out_specs,
        out_shape=out_shape,
        scratch_shapes=[pltpu.VMEM((2, kdim, tn), BF16)],
        compiler_params=_params(("arbitrary", "arbitrary"), blocks, scratch_blocks),
        name=name,
    )(*args)


def _rope_kernel(pos_ref, inv_ref, cos_ref, sin_ref):
    ang = pos_ref[...].astype(F32) * inv_ref[...]
    cos_ref[...] = jnp.cos(ang)
    sin_ref[...] = jnp.sin(ang)


def _rope_tables(positions, *, tm=1024):
    t = positions.size
    half = RET_HEAD_DIM // 2
    inv = ROPE_BASE ** (-np.arange(0, RET_HEAD_DIM, 2, dtype=np.float64) / RET_HEAD_DIM)
    inv = jnp.asarray(inv.astype(np.float32)).reshape(1, half)
    blocks = [((tm, 1), jnp.int32), ((1, half), F32), ((tm, half), F32), ((tm, half), F32)]
    return pl.pallas_call(
        _rope_kernel,
        grid=(t // tm,),
        in_specs=[pl.BlockSpec((tm, 1), lambda i: (i, 0)),
                  pl.BlockSpec((1, half), lambda i: (0, 0))],
        out_specs=[pl.BlockSpec((tm, half), lambda i: (i, 0))] * 2,
        out_shape=[jax.ShapeDtypeStruct((t, half), F32)] * 2,
        compiler_params=_params(("parallel",), blocks),
        name="rope_tables",
    )(positions.reshape(t, 1), inv)


def _s5_prep_kernel(lr_ref, li_ref, ldt_ref, br_ref, bi_ref, cr_ref, ci_ref,
                    bblk_ref, cblk_ref, pn_ref, pp_ref):
    w = S5_TILE_STATES
    lr = lr_ref[...]
    li = li_ref[...]
    dt = jnp.exp(ldt_ref[...])
    a = lr * dt
    th = li * dt
    mag = jnp.exp(a)
    lbr = mag * jnp.cos(th)
    lbi = mag * jnp.sin(th)
    den = lr * lr + li * li
    nr = lbr - 1.0
    coef_r = (nr * lr + lbi * li) / den
    coef_i = (lbi * lr - nr * li) / den
    br = br_ref[...]
    bi = bi_ref[...]
    bbar_r = coef_r * br - coef_i * bi
    bbar_i = coef_r * bi + coef_i * br

    rows = lax.broadcasted_iota(jnp.int32, (V7X_LANES, w), 0)
    cols = lax.broadcasted_iota(jnp.int32, (V7X_LANES, w), 1)
    same_group = ((rows >> (SSM_GROUP.bit_length() - 1))
                  == (cols >> (SSM_STATE.bit_length() - 1)))
    reps = (S5_GROUPS_PER_TILE, 1)

    def blockdiag(v):
        return jnp.where(same_group, jnp.tile(v, reps), 0.0)

    bblk_ref[:, :w] = blockdiag(bbar_r).astype(bblk_ref.dtype)
    bblk_ref[:, w:] = blockdiag(bbar_i).astype(bblk_ref.dtype)
    cblk_ref[:, :w] = blockdiag(cr_ref[...]).astype(cblk_ref.dtype)
    cblk_ref[:, w:] = blockdiag(-ci_ref[...]).astype(cblk_ref.dtype)

    blk = S5_POW_BLOCK
    lo_m = (lax.broadcasted_iota(jnp.int32, (blk, w), 0) + 1).astype(F32)
    hi_m = (lax.broadcasted_iota(jnp.int32, (S5_CHUNK // blk, w), 0) * blk).astype(F32)

    def powers(m, sign):
        mag_m = jnp.exp(sign * (m * a))
        return mag_m * jnp.cos(m * th), sign * (mag_m * jnp.sin(m * th))

    for sign, ref in ((1.0, pp_ref), (-1.0, pn_ref)):
        lo_r, lo_i = powers(lo_m, sign)
        hi_r, hi_i = powers(hi_m, sign)
        for q in range(S5_CHUNK // blk):
            h_re, h_im = hi_r[q:q + 1, :], hi_i[q:q + 1, :]
            ref[q * blk:(q + 1) * blk, :w] = h_re * lo_r - h_im * lo_i
            ref[q * blk:(q + 1) * blk, w:] = h_re * lo_i + h_im * lo_r


def _s5_prep(lam_re, lam_im, log_dt, b_re, b_im, c_re, c_im):
    depth, g, p, h = b_re.shape
    gl, w = S5_GROUPS_PER_TILE, S5_TILE_STATES
    nt = depth * g // gl
    row = lambda v: v.reshape(nt, 1, w)
    bt = lambda v: v.reshape(nt, gl, p, h).transpose(0, 3, 1, 2).reshape(nt, h, w)
    ct = lambda v: v.reshape(nt, gl, h, p).transpose(0, 2, 1, 3).reshape(nt, h, w)
    row_spec = pl.BlockSpec((None, 1, w), lambda j: (j, 0, 0))
    mat_spec = pl.BlockSpec((None, h, w), lambda j: (j, 0, 0))
    blk_spec = pl.BlockSpec((None, V7X_LANES, 2 * w), lambda j: (j, 0, 0))
    pow_spec = pl.BlockSpec((None, S5_CHUNK, 2 * w), lambda j: (j, 0, 0))
    blocks = ([((1, w), F32)] * 3 + [((h, w), F32)] * 4 + [((V7X_LANES, 2 * w), BF16)] * 2
              + [((S5_CHUNK, 2 * w), F32)] * 2)
    return pl.pallas_call(
        _s5_prep_kernel,
        grid=(nt,),
        in_specs=[row_spec] * 3 + [mat_spec] * 4,
        out_specs=[blk_spec, blk_spec, pow_spec, pow_spec],
        out_shape=[jax.ShapeDtypeStruct((nt, V7X_LANES, 2 * w), BF16)] * 2
        + [jax.ShapeDtypeStruct((nt, S5_CHUNK, 2 * w), F32)] * 2,
        compiler_params=_params(("parallel",), blocks),
        name="s5_prep",
    )(row(lam_re), row(lam_im), row(jnp.repeat(log_dt, p)), bt(b_re), bt(b_im), ct(c_re), ct(c_im))


def _s5_kernel(u_ref, bblk_ref, cblk_ref, pn_ref, pp_ref, d_ref, y_ref, *, n_chunks):
    w = S5_TILE_STATES
    lc = S5_CHUNK
    bblk = bblk_ref[...]
    cblk = cblk_ref[...]
    d = d_ref[...]
    r = lax.broadcasted_iota(jnp.int32, (lc, lc), 0)
    c = lax.broadcasted_iota(jnp.int32, (lc, lc), 1)
    tri = (r >= c).astype(BF16)

    def chunk(t, carry):
        cr, ci = carry
        rows = pl.ds(pl.multiple_of(t * lc, lc), lc)
        u = u_ref[rows, :]
        bu = jnp.dot(u.astype(BF16), bblk, preferred_element_type=F32)
        br, bi = bu[:, :w], bu[:, w:]
        pnr, pni = pn_ref[:, :w], pn_ref[:, w:]
        z = jnp.concatenate([pnr * br - pni * bi, pnr * bi + pni * br], axis=1)
        ws = jnp.dot(tri, z.astype(BF16), preferred_element_type=F32)
        wr = ws[:, :w] + cr
        wi = ws[:, w:] + ci
        ppr, ppi = pp_ref[:, :w], pp_ref[:, w:]
        xr = ppr * wr - ppi * wi
        xi = ppr * wi + ppi * wr
        x = jnp.concatenate([xr, xi], axis=1).astype(BF16)
        y = lax.dot_general(x, cblk, (((1,), (1,)), ((), ())), preferred_element_type=F32)
        y_ref[rows, :] = jax.nn.gelu(y + d * u)
        return xr[lc - 1:, :], xi[lc - 1:, :]

    zero = jnp.zeros((1, w), F32)
    lax.fori_loop(0, n_chunks, chunk, (zero, zero), unroll=8)


def _s5_scan(proj, tables, layer, d_skip, bsz, seq):
    bblk, cblk, pn, pp = tables
    nt = SSM_WIDTH // V7X_LANES
    w2 = 2 * S5_TILE_STATES
    tab = lambda rows: pl.BlockSpec((None, rows, w2), lambda j, b: (layer * nt + j, 0, 0))
    blocks = [((seq, V7X_LANES), F32), ((V7X_LANES, w2), BF16), ((V7X_LANES, w2), BF16),
              ((S5_CHUNK, w2), F32), ((S5_CHUNK, w2), F32), ((1, V7X_LANES), F32),
              ((seq, V7X_LANES), F32)]
    return pl.pallas_call(
        functools.partial(_s5_kernel, n_chunks=seq // S5_CHUNK),
        grid=(nt, bsz),
        in_specs=[pl.BlockSpec((seq, V7X_LANES), lambda j, b: (b, j)),
                  tab(V7X_LANES), tab(V7X_LANES), tab(S5_CHUNK), tab(S5_CHUNK),
                  pl.BlockSpec((1, V7X_LANES), lambda j, b: (0, j))],
        out_specs=pl.BlockSpec((seq, V7X_LANES), lambda j, b: (b, j)),
        out_shape=jax.ShapeDtypeStruct((bsz * seq, SSM_WIDTH), F32),
        compiler_params=_params(("parallel", "parallel"), blocks),
        name="s5_scan",
    )(proj, bblk, cblk, pn, pp, d_skip.reshape(1, SSM_WIDTH))


def _glu_kernel(y_ref, w_ref, b_ref, g_ref, o_ref):
    y = y_ref[...]
    z = jnp.dot(y.astype(BF16), w_ref[...], preferred_element_type=F32) + b_ref[...]
    v = y * jax.nn.sigmoid(z)
    ms = jnp.mean(v * v, axis=-1, keepdims=True)
    o_ref[...] = (v * lax.rsqrt(ms + EPS) * g_ref[...]).astype(o_ref.dtype)


def _glu_norm(y, w_glu, layer, b_glu, g_out, *, tm=512):
    t, n = y.shape
    blocks = [((tm, n), F32), ((n, n), BF16), ((1, n), F32), ((1, n), F32), ((tm, n), BF16)]
    vec = pl.BlockSpec((1, n), lambda i: (0, 0))
    return pl.pallas_call(
        _glu_kernel,
        grid=(t // tm,),
        in_specs=[pl.BlockSpec((tm, n), lambda i: (i, 0)),
                  pl.BlockSpec((None, n, n), lambda i: (layer, 0, 0)), vec, vec],
        out_specs=pl.BlockSpec((tm, n), lambda i: (i, 0)),
        out_shape=jax.ShapeDtypeStruct((t, n), BF16),
        compiler_params=_params(("parallel",), blocks),
        name="glu_norm",
    )(y, w_glu, b_glu.reshape(1, n), g_out.reshape(1, n))


def _retention_tables():
    h, c, dh = RET_HEADS, RET_CHUNK, RET_HEAD_DIM
    log_gamma = np.log1p(-np.exp2(-5.0 - np.arange(h, dtype=np.float64)))
    idx = np.arange(c, dtype=np.float64)
    rel = idx[:, None] - idx[None, :]
    mask = np.where(rel >= 0, np.exp(log_gamma[:, None, None] * np.maximum(rel, 0.0)), 0.0)
    cross = np.exp(log_gamma[:, None] * (idx + 1.0))
    state = np.exp(log_gamma[:, None] * (c - 1.0 - idx))
    chunk = np.exp(log_gamma * c)
    k_scale = dh ** -0.5
    assert np.log2(k_scale) == np.round(np.log2(k_scale))
    mask = mask * k_scale
    state = state * k_scale
    wide = lambda v: np.broadcast_to(v[:, :, None], (h, c, dh))
    f = lambda v: jnp.asarray(np.ascontiguousarray(v).astype(np.float32))
    return f(mask), f(wide(cross)), f(wide(state)), f(np.broadcast_to(chunk[:, None, None], (h, 1, dh)))


def _retention_kernel(q_ref, k_ref, v_ref, g_ref, cos_ref, sin_ref, mask_ref, cd_ref, sd_ref,
                      ch_ref, gn_ref, o_ref, state_ref, *, chunks_per_step):
    half = RET_HEAD_DIM // 2
    c = RET_CHUNK
    hh = pl.program_id(2)

    @pl.when(pl.program_id(1) == 0)
    def _():
        state_ref[hh] = jnp.zeros(state_ref.shape[1:], F32)

    mask = mask_ref[hh]
    cross_decay = cd_ref[hh]
    state_decay = sd_ref[hh]
    chunk_decay = ch_ref[hh]
    gn = gn_ref[hh]

    def rotate(t, cos, sin):
        t1, t2 = t[:, :half], t[:, half:]
        return jnp.concatenate([t1 * cos - t2 * sin, t1 * sin + t2 * cos], axis=1)

    for ci in range(chunks_per_step):
        rows = pl.ds(ci * c, c)
        cos = cos_ref[rows, :]
        sin = sin_ref[rows, :]
        q = rotate(q_ref[rows, :], cos, sin)
        k = rotate(k_ref[rows, :], cos, sin)
        v = v_ref[rows, :].astype(BF16)
        qb = q.astype(BF16)
        state = state_ref[hh]
        scores = lax.dot_general(qb, k.astype(BF16), (((1,), (1,)), ((), ())),
                                 preferred_element_type=F32) * mask
        inner = jnp.dot(scores.astype(BF16), v, preferred_element_type=F32)
        cross = jnp.dot(qb, state.astype(BF16), preferred_element_type=F32) * cross_decay
        kv = lax.dot_general((k * state_decay).astype(BF16), v, (((0,), (0,)), ((), ())),
                             preferred_element_type=F32)
        state_ref[hh] = state * chunk_decay + kv
        out = inner + cross
        mu = jnp.mean(out, axis=-1, keepdims=True)
        dev = out - mu
        var = jnp.mean(dev * dev, axis=-1, keepdims=True)
        g = g_ref[rows, :]
        o_ref[rows, :] = (dev * lax.rsqrt(var + EPS) * gn * (g * jax.nn.sigmoid(g))).astype(o_ref.dtype)


def _retention(proj, cos, sin, g_norm, bsz, seq, *, chunks_per_step=8):
    dh, h = RET_HEAD_DIM, RET_HEADS
    rb = chunks_per_step * RET_CHUNK
    steps = seq // rb
    mask, cd, sd, ch = _retention_tables()
    first = SSM_WIDTH // dh
    col = lambda which: pl.BlockSpec(
        (rb, dh), lambda b, s, hh, which=which: (b * steps + s, first + which * h + hh))
    rope = pl.BlockSpec((rb, dh // 2), lambda b, s, hh: (b * steps + s, 0))
    whole = lambda rows, cols: pl.BlockSpec((h, rows, cols), lambda b, s, hh: (0, 0, 0))
    blocks = ([((rb, dh), F32)] * 4 + [((rb, dh // 2), F32)] * 2
              + [((h, RET_CHUNK, RET_CHUNK), F32), ((h, RET_CHUNK, dh), F32),
                 ((h, RET_CHUNK, dh), F32), ((h, 1, dh), F32), ((h, 1, dh), F32), ((rb, dh), BF16)])
    return pl.pallas_call(
        functools.partial(_retention_kernel, chunks_per_step=chunks_per_step),
        grid=(bsz, steps, h),
        in_specs=[col(0), col(1), col(2), col(3), rope, rope,
                  whole(RET_CHUNK, RET_CHUNK), whole(RET_CHUNK, dh), whole(RET_CHUNK, dh),
                  whole(1, dh), whole(1, dh)],
        out_specs=pl.BlockSpec((rb, dh), lambda b, s, hh: (b * steps + s, hh)),
        out_shape=jax.ShapeDtypeStruct((bsz * seq, RET_WIDTH), BF16),
        scratch_shapes=[pltpu.VMEM((h, dh, dh), F32)],
        compiler_params=_params(("parallel", "arbitrary", "arbitrary"), blocks,
                                [((h, dh, dh), F32)]),
        name="retention",
    )(proj, proj, proj, proj, cos, sin, mask, cd, sd, ch, g_norm.reshape(h, 1, dh))


def kernel(x, c, positions, w_ada, b_ada, g_mix, w_in, lam_re, lam_im, log_dt, b_re, b_im, c_re,
           c_im, d_skip, w_glu, b_glu, g_ssm_out, g_ret_norm, w_out, g_mlp, w_up, w_down, g_final):
    bsz, seq, d = x.shape
    depth = w_ada.shape[0]
    x2 = x.reshape(bsz * seq, d)
    c_pad = jnp.zeros((ADA_ROWS, d), F32).at[:bsz].set(c)
    b_ada3 = b_ada.reshape(depth, 1, b_ada.shape[1])
    mod = _ada_mod(c_pad, w_ada, b_ada3, 0)
    cos, sin = _rope_tables(positions)
    tables = _s5_prep(lam_re, lam_im, log_dt, b_re, b_im, c_re, c_im)
    w_glu_b = w_glu.astype(BF16)
    for i in range(depth):
        shift1, scale1, gate1, shift2, scale2, gate2 = jnp.split(mod[:bsz], 6, axis=-1)

        h = _normmod(x2, g_mix[i], scale1, shift1, seq)
        proj, w_out_b = _matmul_cast(h, w_in, i, tm=1024, tn=1024, out_dtype=F32,
                                     side=w_out, side_rows=64, name="in_proj")
        y = _s5_scan(proj, tables, i, d_skip[i], bsz, seq)
        ssm_out = _glu_norm(y, w_glu_b, i, b_glu[i], g_ssm_out[i])
        ret_out = _retention(proj, cos, sin, g_ret_norm[i], bsz, seq)
        x2 = _matmul([ssm_out, ret_out], w_out_b[None], 0, tm=1024, tn=1024, tk=d, out_dtype=F32,
                     epilogue=_epi_resid, resid=x2, gate=gate1, seq=seq, name="out_proj")

        h = _normmod(x2, g_mlp[i], scale2, shift2, seq)
        a, w_down_b = _matmul_cast(h, w_up, i, tm=1024, tn=1024, out_dtype=BF16,
                                   epilogue=_epi_relu2, side=w_down, side_rows=128, name="mlp_up")
        ada = (c_pad, w_ada, b_ada3, i + 1) if i + 1 < depth else None
        res = _matmul([a], w_down_b[None], 0, tm=1024, tn=1024, tk=2048, out_dtype=F32,
                      epilogue=_epi_resid, resid=x2, gate=gate2, seq=seq, ada=ada, name="mlp_down")
        x2, mod = res if ada is not None else (res, None)
    return _rmsnorm(x2, g_final).reshape(bsz, seq, d)
```

```python
import functools

import numpy as np
import jax
import jax.numpy as jnp
from jax import lax
from jax.experimental import pallas as pl
from jax.experimental.pallas import tpu as pltpu

F32 = jnp.float32
BF16 = jnp.bfloat16

SSM_WIDTH = 2048
RET_WIDTH = 2048
SSM_GROUP = 16
SSM_STATE = 64
RET_HEAD_DIM = 256
RET_HEADS = 8
RET_CHUNK = 128
ROPE_BASE = 10000.0
EPS = 1e-6

V7X_LANES = 128
V7X_VMEM_BYTES = 64 * 1024 * 1024
V7X_COMPILER_SCRATCH_BYTES = 6 * 1024 * 1024

S5_CHUNK = 256
S5_POW_BLOCK = 16
S5_GROUPS_PER_TILE = V7X_LANES // SSM_GROUP
S5_TILE_STATES = S5_GROUPS_PER_TILE * SSM_STATE


def _nbytes(shape, dtype):
    return int(np.prod(shape)) * jnp.dtype(dtype).itemsize


def _params(semantics, pipelined_blocks, scratch_blocks=()):
    need = 2 * sum(_nbytes(s, d) for s, d in pipelined_blocks)
    need += sum(_nbytes(s, d) for s, d in scratch_blocks)
    limit = min(need + V7X_COMPILER_SCRATCH_BYTES, V7X_VMEM_BYTES - 4 * 1024 * 1024)
    return pltpu.CompilerParams(dimension_semantics=semantics, vmem_limit_bytes=limit)


def _ada_kernel(c_ref, w_ref, b_ref, o_ref):
    c = c_ref[...]
    c_act = c * jax.nn.sigmoid(c)
    o_ref[...] = jnp.dot(c_act.astype(BF16), w_ref[...].astype(BF16),
                         preferred_element_type=F32) + b_ref[...]


ADA_ROWS = 8
ADA_SIDE_K = 512
ADA_SIDE_N = 1024


def _ada_side(c_ref, w_ref, b_ref, mod_ref, step):
    kc, nc = w_ref.shape
    n_kb = c_ref.shape[1] // kc
    n_tiles = n_kb * (mod_ref.shape[1] // nc)

    @pl.when(step < n_tiles)
    def _():
        kb = step % n_kb
        c = c_ref[:, pl.ds(pl.multiple_of(kb * kc, kc), kc)]
        c_act = c * jax.nn.sigmoid(c)
        part = jnp.dot(c_act.astype(BF16), w_ref[...].astype(BF16), preferred_element_type=F32)
        cols = pl.ds(pl.multiple_of((step // n_kb) * nc, nc), nc)

        @pl.when(kb == 0)
        def _():
            mod_ref[:, cols] = part + b_ref[:, cols]

        @pl.when(kb > 0)
        def _():
            mod_ref[:, cols] += part


def _ada_mod(c_pad, w_ada, b_ada3, layer, *, tn=1024):
    _, d, n = w_ada.shape
    rows = ADA_ROWS
    blocks = [((rows, d), F32), ((d, tn), F32), ((1, tn), F32), ((rows, tn), F32)]
    return pl.pallas_call(
        _ada_kernel,
        grid=(n // tn,),
        in_specs=[pl.BlockSpec((rows, d), lambda j: (0, 0)),
                  pl.BlockSpec((None, d, tn), lambda j: (layer, 0, j)),
                  pl.BlockSpec((None, 1, tn), lambda j: (layer, 0, j))],
        out_specs=pl.BlockSpec((rows, tn), lambda j: (0, j)),
        out_shape=jax.ShapeDtypeStruct((rows, n), F32),
        compiler_params=_params(("parallel",), blocks),
        name="ada_mod",
    )(c_pad, w_ada, b_ada3)


def _normmod_kernel(x_ref, g_ref, sc_ref, sh_ref, o_ref):
    x = x_ref[...]
    ms = jnp.mean(x * x, axis=-1, keepdims=True)
    y = x * lax.rsqrt(ms + EPS) * g_ref[...]
    o_ref[...] = (y * (1.0 + sc_ref[...]) + sh_ref[...]).astype(o_ref.dtype)


def _normmod(x2, g, scale, shift, seq, *, tm=512):
    t, d = x2.shape
    bsz = scale.shape[0]
    blocks = [((tm, d), F32), ((1, d), F32), ((1, d), F32), ((1, d), F32), ((tm, d), BF16)]
    per_b = pl.BlockSpec((None, 1, d), lambda i: (i * tm // seq, 0, 0))
    return pl.pallas_call(
        _normmod_kernel,
        grid=(t // tm,),
        in_specs=[pl.BlockSpec((tm, d), lambda i: (i, 0)),
                  pl.BlockSpec((1, d), lambda i: (0, 0)), per_b, per_b],
        out_specs=pl.BlockSpec((tm, d), lambda i: (i, 0)),
        out_shape=jax.ShapeDtypeStruct((t, d), BF16),
        compiler_params=_params(("parallel",), blocks, [((tm, d), F32)]),
        name="normmod",
    )(x2, g.reshape(1, d), scale.reshape(bsz, 1, d), shift.reshape(bsz, 1, d))


def _rmsnorm_kernel(x_ref, g_ref, o_ref):
    x = x_ref[...]
    ms = jnp.mean(x * x, axis=-1, keepdims=True)
    o_ref[...] = x * lax.rsqrt(ms + EPS) * g_ref[...]


def _rmsnorm(x2, g, *, tm=512):
    t, d = x2.shape
    blocks = [((tm, d), F32), ((1, d), F32), ((tm, d), F32)]
    return pl.pallas_call(
        _rmsnorm_kernel,
        grid=(t // tm,),
        in_specs=[pl.BlockSpec((tm, d), lambda i: (i, 0)),
                  pl.BlockSpec((1, d), lambda i: (0, 0))],
        out_specs=pl.BlockSpec((tm, d), lambda i: (i, 0)),
        out_shape=jax.ShapeDtypeStruct((t, d), F32),
        compiler_params=_params(("parallel",), blocks, [((tm, d), F32)]),
        name="final_rmsnorm",
    )(x2, g.reshape(1, d))


def _epi_none(acc):
    return acc


def _epi_relu2(acc):
    r = jnp.maximum(acc, 0.0)
    return r * r


def _epi_resid(acc, x_ref, gate_ref):
    return x_ref[...] + gate_ref[...] * acc


def _mm_kernel(*refs, n_a, nk, epilogue, has_ada):
    a_refs, w_ref, rest = refs[:n_a], refs[n_a], refs[n_a + 1:]
    if has_ada:
        n_main = len(rest) - (5 if nk == 1 else 6)
        c_ref, wa_ref, ba_ref = rest[n_main:n_main + 3]
        mod_ref = rest[n_main + 4]
        step = ((pl.program_id(0) * pl.num_programs(1) + pl.program_id(1)) * nk
                + pl.program_id(2))
        _ada_side(c_ref, wa_ref, ba_ref, mod_ref, step)
        rest = rest[:n_main] + (rest[n_main + 3],) + rest[n_main + 5:]

    def product():
        total, row = None, 0
        for a_ref in a_refs:
            kk = a_ref.shape[1]
            part = jnp.dot(a_ref[...], w_ref[row:row + kk, :], preferred_element_type=F32)
            total = part if total is None else total + part
            row += kk
        return total

    if nk == 1:
        *extra, o_ref = rest
        o_ref[...] = epilogue(product(), *extra).astype(o_ref.dtype)
        return
    *extra, o_ref, acc_ref = rest
    k = pl.program_id(2)

    @pl.when(k == 0)
    def _():
        acc_ref[...] = product()

    @pl.when(jnp.logical_and(k > 0, k < nk - 1))
    def _():
        acc_ref[...] += product()

    @pl.when(k == nk - 1)
    def _():
        o_ref[...] = epilogue(acc_ref[...] + product(), *extra).astype(o_ref.dtype)


def _matmul(a_list, w, layer, *, tm, tn, tk, out_dtype, epilogue=_epi_none, resid=None,
            gate=None, seq=None, ada=None, name):
    m = a_list[0].shape[0]
    _, kdim, n = w.shape
    nk = kdim // tk
    if len(a_list) == 1:
        in_specs = [pl.BlockSpec((tm, tk), lambda i, j, k: (i, k))]
        blocks = [((tm, tk), a_list[0].dtype)]
    else:
        assert nk == 1 and sum(a.shape[1] for a in a_list) == kdim
        in_specs = [pl.BlockSpec((tm, a.shape[1]), lambda i, j, k: (i, 0)) for a in a_list]
        blocks = [((tm, a.shape[1]), a.dtype) for a in a_list]
    in_specs.append(pl.BlockSpec((None, tk, tn), lambda i, j, k: (layer, k, j)))
    blocks += [((tk, tn), w.dtype), ((tm, tn), out_dtype)]
    args = list(a_list) + [w]
    if resid is not None:
        bsz = gate.shape[0]
        in_specs += [pl.BlockSpec((tm, tn), lambda i, j, k: (i, j)),
                     pl.BlockSpec((None, 1, tn), lambda i, j, k: (i * tm // seq, 0, j))]
        blocks += [((tm, tn), F32), ((1, tn), F32)]
        args += [resid, gate.reshape(bsz, 1, n)]
    scratch = [] if nk == 1 else [pltpu.VMEM((tm, tn), F32)]
    scratch_blocks = [((tm, tn), F32)] * (1 if nk == 1 else 2)
    out_specs = pl.BlockSpec((tm, tn), lambda i, j, k: (i, j))
    out_shape = jax.ShapeDtypeStruct((m, n), out_dtype)
    if ada is not None:
        c_pad, w_ada, b_ada3, ada_layer = ada
        _, d_ada, n_ada = w_ada.shape
        n_kb, n_cb = d_ada // ADA_SIDE_K, n_ada // ADA_SIDE_N
        nj = n // tn
        assert n_kb * n_cb <= (m // tm) * nj * nk
        tile = lambda i, j, k: jnp.minimum((i * nj + j) * nk + k, n_kb * n_cb - 1)
        in_specs += [pl.BlockSpec((ADA_ROWS, d_ada), lambda i, j, k: (0, 0)),
                     pl.BlockSpec((None, ADA_SIDE_K, ADA_SIDE_N),
                                  lambda i, j, k: (ada_layer, tile(i, j, k) % n_kb, tile(i, j, k) // n_kb)),
                     pl.BlockSpec((None, 1, n_ada), lambda i, j, k: (ada_layer, 0, 0))]
        out_specs = [out_specs, pl.BlockSpec((ADA_ROWS, n_ada), lambda i, j, k: (0, 0))]
        out_shape = [out_shape, jax.ShapeDtypeStruct((ADA_ROWS, n_ada), F32)]
        blocks += [((ADA_ROWS, d_ada), F32), ((ADA_SIDE_K, ADA_SIDE_N), F32), ((1, n_ada), F32),
                   ((ADA_ROWS, n_ada), F32)]
        args += [c_pad, w_ada, b_ada3]
    return pl.pallas_call(
        functools.partial(_mm_kernel, n_a=len(a_list), nk=nk, epilogue=epilogue,
                          has_ada=ada is not None),
        grid=(m // tm, n // tn, nk),
        in_specs=in_specs,
        out_specs=out_specs,
        out_shape=out_shape,
        scratch_shapes=scratch,
        compiler_params=_params(("arbitrary",) * 3 if ada is not None
                                else ("parallel", "parallel", "arbitrary"), blocks, scratch_blocks),
        name=name,
    )(*args)


def _mm_cast_kernel(a_ref, wc_ref, *rest, epilogue, has_side):
    if has_side:
        side_ref, o_ref, side_o_ref, wb_ref = rest
        side_o_ref[...] = side_ref[...].astype(BF16)
    else:
        o_ref, wb_ref = rest
    j = pl.program_id(0)
    i = pl.program_id(1)
    nj = pl.num_programs(0) - 1
    rows = wc_ref.shape[0]

    @pl.when(j < nj)
    def _():
        wb_ref[j % 2, pl.ds(pl.multiple_of(i * rows, rows), rows), :] = wc_ref[...].astype(BF16)

    @pl.when(j > 0)
    def _():
        prod = jnp.dot(a_ref[...], wb_ref[(j - 1) % 2], preferred_element_type=F32)
        o_ref[...] = epilogue(prod).astype(o_ref.dtype)


def _matmul_cast(a, w, layer, *, tm, tn, out_dtype, epilogue=_epi_none, side=None,
                 side_rows=None, name):
    m, kdim = a.shape
    n = w.shape[2]
    ni, nj = m // tm, n // tn
    rows = kdim // ni
    row_blk = lambda j, i: jnp.where(j > 0, i, 0)
    col_blk = lambda j: jnp.maximum(j - 1, 0)
    blocks = [((tm, kdim), a.dtype), ((rows, tn), w.dtype), ((tm, tn), out_dtype)]
    scratch_blocks = [((2, kdim, tn), BF16), ((tm, tn), F32)]
    in_specs = [pl.BlockSpec((tm, kdim), lambda j, i: (row_blk(j, i), 0)),
                pl.BlockSpec((None, rows, tn),
                             lambda j, i: (layer, jnp.where(j < nj, i, ni - 1),
                                           jnp.minimum(j, nj - 1)))]
    out_specs = pl.BlockSpec((tm, tn), lambda j, i: (row_blk(j, i), col_blk(j)))
    out_shape = jax.ShapeDtypeStruct((m, n), out_dtype)
    args = [a, w]
    if side is not None:
        _, r, c = side.shape
        n_side = r // side_rows
        assert n_side * side_rows == r and n_side <= (nj + 1) * ni
        side_blk = lambda j, i: jnp.minimum(j * ni + i, n_side - 1)
        in_specs.append(pl.BlockSpec((None, side_rows, c), lambda j, i: (layer, side_blk(j, i), 0)))
        out_specs = [out_specs, pl.BlockSpec((side_rows, c), lambda j, i: (side_blk(j, i), 0))]
        out_shape = [out_shape, jax.ShapeDtypeStruct((r, c), BF16)]
        blocks += [((side_rows, c), F32), ((side_rows, c), BF16)]
        args.append(side)
    return pl.pallas_call(
        functools.partial(_mm_cast_kernel, epilogue=epilogue, has_side=side is not None),
        grid=(nj + 1, ni),
        in_specs=in_specs,
        out_specs=out_specs,
        out_shape=out_shape,
        scratch_shapes=[pltpu.VMEM((2, kdim, tn), BF16)],
        compiler_params=_params(("arbitrary", "arbitrary"), blocks, scratch_blocks),
        name=name,
    )(*args)


def _rope_kernel(pos_ref, inv_ref, cos_ref, sin_ref):
    ang = pos_ref[...].astype(F32) * inv_ref[...]
    cos_ref[...] = jnp.cos(ang)
    sin_ref[...] = jnp.sin(ang)


def _rope_tables(positions, *, tm=1024):
    t = positions.size
    half = RET_HEAD_DIM // 2
    inv = ROPE_BASE ** (-np.arange(0, RET_HEAD_DIM, 2, dtype=np.float64) / RET_HEAD_DIM)
    inv = jnp.asarray(inv.astype(np.float32)).reshape(1, half)
    blocks = [((tm, 1), jnp.int32), ((1, half), F32), ((tm, half), F32), ((tm, half), F32)]
    return pl.pallas_call(
        _rope_kernel,
        grid=(t // tm,),
        in_specs=[pl.BlockSpec((tm, 1), lambda i: (i, 0)),
                  pl.BlockSpec((1, half), lambda i: (0, 0))],
        out_specs=[pl.BlockSpec((tm, half), lambda i: (i, 0))] * 2,
        out_shape=[jax.ShapeDtypeStruct((t, half), F32)] * 2,
        compiler_params=_params(("parallel",), blocks),
        name="rope_tables",
    )(positions.reshape(t, 1), inv)


def _s5_prep_kernel(lr_ref, li_ref, ldt_ref, br_ref, bi_ref, cr_ref, ci_ref,
                    bblk_ref, cblk_ref, pn_ref, pp_ref):
    w = S5_TILE_STATES
    lr = lr_ref[...]
    li = li_ref[...]
    dt = jnp.exp(ldt_ref[...])
    a = lr * dt
    th = li * dt
    mag = jnp.exp(a)
    lbr = mag * jnp.cos(th)
    lbi = mag * jnp.sin(th)
    den = lr * lr + li * li
    nr = lbr - 1.0
    coef_r = (nr * lr + lbi * li) / den
    coef_i = (lbi * lr - nr * li) / den
    br = br_ref[...]
    bi = bi_ref[...]
    bbar_r = coef_r * br - coef_i * bi
    bbar_i = coef_r * bi + coef_i * br

    rows = lax.broadcasted_iota(jnp.int32, (V7X_LANES, w), 0)
    cols = lax.broadcasted_iota(jnp.int32, (V7X_LANES, w), 1)
    same_group = ((rows >> (SSM_GROUP.bit_length() - 1))
                  == (cols >> (SSM_STATE.bit_length() - 1)))
    reps = (S5_GROUPS_PER_TILE, 1)

    def blockdiag(v):
        return jnp.where(same_group, jnp.tile(v, reps), 0.0)

    bblk_ref[:, :w] = blockdiag(bbar_r).astype(bblk_ref.dtype)
    bblk_ref[:, w:] = blockdiag(bbar_i).astype(bblk_ref.dtype)
    cblk_ref[:, :w] = blockdiag(cr_ref[...]).astype(cblk_ref.dtype)
    cblk_ref[:, w:] = blockdiag(-ci_ref[...]).astype(cblk_ref.dtype)

    blk = S5_POW_BLOCK
    lo_m = (lax.broadcasted_iota(jnp.int32, (blk, w), 0) + 1).astype(F32)
    hi_m = (lax.broadcasted_iota(jnp.int32, (S5_CHUNK // blk, w), 0) * blk).astype(F32)

    def powers(m, sign):
        mag_m = jnp.exp(sign * (m * a))
        return mag_m * jnp.cos(m * th), sign * (mag_m * jnp.sin(m * th))

    for sign, ref in ((1.0, pp_ref), (-1.0, pn_ref)):
        lo_r, lo_i = powers(lo_m, sign)
        hi_r, hi_i = powers(hi_m, sign)
        for q in range(S5_CHUNK // blk):
            h_re, h_im = hi_r[q:q + 1, :], hi_i[q:q + 1, :]
            ref[q * blk:(q + 1) * blk, :w] = h_re * lo_r - h_im * lo_i
            ref[q * blk:(q + 1) * blk, w:] = h_re * lo_i + h_im * lo_r


def _s5_prep(lam_re, lam_im, log_dt, b_re, b_im, c_re, c_im):
    depth, g, p, h = b_re.shape
    gl, w = S5_GROUPS_PER_TILE, S5_TILE_STATES
    nt = depth * g // gl
    row = lambda v: v.reshape(nt, 1, w)
    bt = lambda v: v.reshape(nt, gl, p, h).transpose(0, 3, 1, 2).reshape(nt, h, w)
    ct = lambda v: v.reshape(nt, gl, h, p).transpose(0, 2, 1, 3).reshape(nt, h, w)
    row_spec = pl.BlockSpec((None, 1, w), lambda j: (j, 0, 0))
    mat_spec = pl.BlockSpec((None, h, w), lambda j: (j, 0, 0))
    blk_spec = pl.BlockSpec((None, V7X_LANES, 2 * w), lambda j: (j, 0, 0))
    pow_spec = pl.BlockSpec((None, S5_CHUNK, 2 * w), lambda j: (j, 0, 0))
    blocks = ([((1, w), F32)] * 3 + [((h, w), F32)] * 4 + [((V7X_LANES, 2 * w), BF16)] * 2
              + [((S5_CHUNK, 2 * w), F32)] * 2)
    return pl.pallas_call(
        _s5_prep_kernel,
        grid=(nt,),
        in_specs=[row_spec] * 3 + [mat_spec] * 4,
        out_specs=[blk_spec, blk_spec, pow_spec, pow_spec],
        out_shape=[jax.ShapeDtypeStruct((nt, V7X_LANES, 2 * w), BF16)] * 2
        + [jax.ShapeDtypeStruct((nt, S5_CHUNK, 2 * w), F32)] * 2,
        compiler_params=_params(("parallel",), blocks),
        name="s5_prep",
    )(row(lam_re), row(lam_im), row(jnp.repeat(log_dt, p)), bt(b_re), bt(b_im), ct(c_re), ct(c_im))


def _s5_kernel(u_ref, bblk_ref, cblk_ref, pn_ref, pp_ref, d_ref, y_ref, *, n_chunks):
    w = S5_TILE_STATES
    lc = S5_CHUNK
    bblk = bblk_ref[...]
    cblk = cblk_ref[...]
    d = d_ref[...]
    r = lax.broadcasted_iota(jnp.int32, (lc, lc), 0)
    c = lax.broadcasted_iota(jnp.int32, (lc, lc), 1)
    tri = (r >= c).astype(BF16)

    def chunk(t, carry):
        cr, ci = carry
        rows = pl.ds(pl.multiple_of(t * lc, lc), lc)
        u = u_ref[rows, :]
        bu = jnp.dot(u.astype(BF16), bblk, preferred_element_type=F32)
        br, bi = bu[:, :w], bu[:, w:]
        pnr, pni = pn_ref[:, :w], pn_ref[:, w:]
        z = jnp.concatenate([pnr * br - pni * bi, pnr * bi + pni * br], axis=1)
        ws = jnp.dot(tri, z.astype(BF16), preferred_element_type=F32)
        wr = ws[:, :w] + cr
        wi = ws[:, w:] + ci
        ppr, ppi = pp_ref[:, :w], pp_ref[:, w:]
        xr = ppr * wr - ppi * wi
        xi = ppr * wi + ppi * wr
        x = jnp.concatenate([xr, xi], axis=1).astype(BF16)
        y = lax.dot_general(x, cblk, (((1,), (1,)), ((), ())), preferred_element_type=F32)
        y_ref[rows, :] = jax.nn.gelu(y + d * u)
        return xr[lc - 1:, :], xi[lc - 1:, :]

    zero = jnp.zeros((1, w), F32)
    lax.fori_loop(0, n_chunks, chunk, (zero, zero), unroll=8)


def _s5_scan(proj, tables, layer, d_skip, bsz, seq):
    bblk, cblk, pn, pp = tables
    nt = SSM_WIDTH // V7X_LANES
    w2 = 2 * S5_TILE_STATES
    tab = lambda rows: pl.BlockSpec((None, rows, w2), lambda j, b: (layer * nt + j, 0, 0))
    blocks = [((seq, V7X_LANES), F32), ((V7X_LANES, w2), BF16), ((V7X_LANES, w2), BF16),
              ((S5_CHUNK, w2), F32), ((S5_CHUNK, w2), F32), ((1, V7X_LANES), F32),
              ((seq, V7X_LANES), F32)]
    return pl.pallas_call(
        functools.partial(_s5_kernel, n_chunks=seq // S5_CHUNK),
        grid=(nt, bsz),
        in_specs=[pl.BlockSpec((seq, V7X_LANES), lambda j, b: (b, j)),
                  tab(V7X_LANES), tab(V7X_LANES), tab(S5_CHUNK), tab(S5_CHUNK),
                  pl.BlockSpec((1, V7X_LANES), lambda j, b: (0, j))],
        out_specs=pl.BlockSpec((seq, V7X_LANES), lambda j, b: (b, j)),
        out_shape=jax.ShapeDtypeStruct((bsz * seq, SSM_WIDTH), F32),
        compiler_params=_params(("parallel", "parallel"), blocks),
        name="s5_scan",
    )(proj, bblk, cblk, pn, pp, d_skip.reshape(1, SSM_WIDTH))


def _glu_kernel(y_ref, w_ref, b_ref, g_ref, o_ref):
    y = y_ref[...]
    z = jnp.dot(y.astype(BF16), w_ref[...], preferred_element_type=F32) + b_ref[...]
    v = y * jax.nn.sigmoid(z)
    ms = jnp.mean(v * v, axis=-1, keepdims=True)
    o_ref[...] = (v * lax.rsqrt(ms + EPS) * g_ref[...]).astype(o_ref.dtype)


def _glu_norm(y, w_glu, layer, b_glu, g_out, *, tm=512):
    t, n = y.shape
    blocks = [((tm, n), F32), ((n, n), BF16), ((1, n), F32), ((1, n), F32), ((tm, n), BF16)]
    vec = pl.BlockSpec((1, n), lambda i: (0, 0))
    return pl.pallas_call(
        _glu_kernel,
        grid=(t // tm,),
        in_specs=[pl.BlockSpec((tm, n), lambda i: (i, 0)),
                  pl.BlockSpec((None, n, n), lambda i: (layer, 0, 0)), vec, vec],
        out_specs=pl.BlockSpec((tm, n), lambda i: (i, 0)),
        out_shape=jax.ShapeDtypeStruct((t, n), BF16),
        compiler_params=_params(("parallel",), blocks),
        name="glu_norm",
    )(y, w_glu, b_glu.reshape(1, n), g_out.reshape(1, n))


def _retention_tables():
    h, c, dh = RET_HEADS, RET_CHUNK, RET_HEAD_DIM
    log_gamma = np.log1p(-np.exp2(-5.0 - np.arange(h, dtype=np.float64)))
    idx = np.arange(c, dtype=np.float64)
    rel = idx[:, None] - idx[None, :]
    mask = np.where(rel >= 0, np.exp(log_gamma[:, None, None] * np.maximum(rel, 0.0)), 0.0)
    cross = np.exp(log_gamma[:, None] * (idx + 1.0))
    state = np.exp(log_gamma[:, None] * (c - 1.0 - idx))
    chunk = np.exp(log_gamma * c)
    k_scale = dh ** -0.5
    assert np.log2(k_scale) == np.round(np.log2(k_scale))
    mask = mask * k_scale
    state = state * k_scale
    wide = lambda v: np.broadcast_to(v[:, :, None], (h, c, dh))
    f = lambda v: jnp.asarray(np.ascontiguousarray(v).astype(np.float32))
    return f(mask), f(wide(cross)), f(wide(state)), f(np.broadcast_to(chunk[:, None, None], (h, 1, dh)))


def _retention_kernel(q_ref, k_ref, v_ref, g_ref, cos_ref, sin_ref, mask_ref, cd_ref, sd_ref,
                      ch_ref, gn_ref, o_ref, state_ref, *, chunks_per_step, heads_per_step):
    half = RET_HEAD_DIM // 2
    dh = RET_HEAD_DIM
    c = RET_CHUNK
    head0 = pl.program_id(2) * heads_per_step

    @pl.when(pl.program_id(1) == 0)
    def _():
        for e in range(heads_per_step):
            state_ref[head0 + e] = jnp.zeros(state_ref.shape[1:], F32)

    def rotate(t, cos, sin):
        t1, t2 = t[:, :half], t[:, half:]
        return jnp.concatenate([t1 * cos - t2 * sin, t1 * sin + t2 * cos], axis=1)

    for ci in range(chunks_per_step):
        rows = pl.ds(ci * c, c)
        cos = cos_ref[rows, :]
        sin = sin_ref[rows, :]
        for e in range(heads_per_step):
            hh = head0 + e
            cols = slice(e * dh, (e + 1) * dh)
            q = rotate(q_ref[rows, cols], cos, sin)
            k = rotate(k_ref[rows, cols], cos, sin)
            v = v_ref[rows, cols].astype(BF16)
            qb = q.astype(BF16)
            state = state_ref[hh]
            scores = lax.dot_general(qb, k.astype(BF16), (((1,), (1,)), ((), ())),
                                     preferred_element_type=F32) * mask_ref[hh]
            inner = jnp.dot(scores.astype(BF16), v, preferred_element_type=F32)
            cross = jnp.dot(qb, state.astype(BF16), preferred_element_type=F32) * cd_ref[hh]
            kv = lax.dot_general((k * sd_ref[hh]).astype(BF16), v, (((0,), (0,)), ((), ())),
                                 preferred_element_type=F32)
            state_ref[hh] = state * ch_ref[hh] + kv
            out = inner + cross
            mu = jnp.mean(out, axis=-1, keepdims=True)
            dev = out - mu
            var = jnp.mean(dev * dev, axis=-1, keepdims=True)
            g = g_ref[rows, cols]
            o_ref[rows, cols] = (dev * lax.rsqrt(var + EPS) * gn_ref[hh]
                                 * (g * jax.nn.sigmoid(g))).astype(o_ref.dtype)


def _retention(proj, cos, sin, g_norm, bsz, seq, *, chunks_per_step=8, heads_per_step=2):
    dh, h = RET_HEAD_DIM, RET_HEADS
    rb = chunks_per_step * RET_CHUNK
    steps = seq // rb
    wide = heads_per_step * dh
    groups = h // heads_per_step
    mask, cd, sd, ch = _retention_tables()
    first = SSM_WIDTH // wide
    col = lambda which: pl.BlockSpec(
        (rb, wide), lambda b, s, hg, which=which: (b * steps + s, first + which * groups + hg))
    rope = pl.BlockSpec((rb, dh // 2), lambda b, s, hg: (b * steps + s, 0))
    whole = lambda rows, cols: pl.BlockSpec((h, rows, cols), lambda b, s, hg: (0, 0, 0))
    blocks = ([((rb, wide), F32)] * 4 + [((rb, dh // 2), F32)] * 2
              + [((h, RET_CHUNK, RET_CHUNK), F32), ((h, RET_CHUNK, dh), F32),
                 ((h, RET_CHUNK, dh), F32), ((h, 1, dh), F32), ((h, 1, dh), F32), ((rb, wide), BF16)])
    return pl.pallas_call(
        functools.partial(_retention_kernel, chunks_per_step=chunks_per_step,
                          heads_per_step=heads_per_step),
        grid=(bsz, steps, groups),
        in_specs=[col(0), col(1), col(2), col(3), rope, rope,
                  whole(RET_CHUNK, RET_CHUNK), whole(RET_CHUNK, dh), whole(RET_CHUNK, dh),
                  whole(1, dh), whole(1, dh)],
        out_specs=pl.BlockSpec((rb, wide), lambda b, s, hg: (b * steps + s, hg)),
        out_shape=jax.ShapeDtypeStruct((bsz * seq, RET_WIDTH), BF16),
        scratch_shapes=[pltpu.VMEM((h, dh, dh), F32)],
        compiler_params=_params(("parallel", "arbitrary", "arbitrary"), blocks,
                                [((h, dh, dh), F32)]),
        name="retention",
    )(proj, proj, proj, proj, cos, sin, mask, cd, sd, ch, g_norm.reshape(h, 1, dh))


def kernel(x, c, positions, w_ada, b_ada, g_mix, w_in, lam_re, lam_im, log_dt, b_re, b_im, c_re,
           c_im, d_skip, w_glu, b_glu, g_ssm_out, g_ret_norm, w_out, g_mlp, w_up, w_down, g_final):
    bsz, seq, d = x.shape
    depth = w_ada.shape[0]
    x2 = x.reshape(bsz * seq, d)
    c_pad = jnp.zeros((ADA_ROWS, d), F32).at[:bsz].set(c)
    b_ada3 = b_ada.reshape(depth, 1, b_ada.shape[1])
    mod = _ada_mod(c_pad, w_ada, b_ada3, 0)
    cos, sin = _rope_tables(positions)
    tables = _s5_prep(lam_re, lam_im, log_dt, b_re, b_im, c_re, c_im)
    w_glu_b = w_glu.astype(BF16)
    for i in range(depth):
        shift1, scale1, gate1, shift2, scale2, gate2 = jnp.split(mod[:bsz], 6, axis=-1)

        h = _normmod(x2, g_mix[i], scale1, shift1, seq)
        proj, w_out_b = _matmul_cast(h, w_in, i, tm=1024, tn=1024, out_dtype=F32,
                                     side=w_out, side_rows=64, name="in_proj")
        y = _s5_scan(proj, tables, i, d_skip[i], bsz, seq)
        ssm_out = _glu_norm(y, w_glu_b, i, b_glu[i], g_ssm_out[i])
        ret_out = _retention(proj, cos, sin, g_ret_norm[i], bsz, seq)
        x2 = _matmul([ssm_out, ret_out], w_out_b[None], 0, tm=1024, tn=1024, tk=d, out_dtype=F32,
                     epilogue=_epi_resid, resid=x2, gate=gate1, seq=seq, name="out_proj")

        h = _normmod(x2, g_mlp[i], scale2, shift2, seq)
        a, w_down_b = _matmul_cast(h, w_up, i, tm=1024, tn=1024, out_dtype=BF16,
                                   epilogue=_epi_relu2, side=w_down, side_rows=128, name="mlp_up")
        ada = (c_pad, w_ada, b_ada3, i + 1) if i + 1 < depth else None
        res = _matmul([a], w_down_b[None], 0, tm=1024, tn=1024, tk=2048, out_dtype=F32,
                      epilogue=_epi_resid, resid=x2, gate=gate2, seq=seq, ada=ada, name="mlp_down")
        x2, mod = res if ada is not None else (res, None)
    return _rmsnorm(x2, g_final).reshape(bsz, seq, d)
```

```python
import functools

import numpy as np
import jax
import jax.numpy as jnp
from jax import lax
from jax.experimental import pallas as pl
from jax.experimental.pallas import tpu as pltpu

F32 = jnp.float32
BF16 = jnp.bfloat16

SSM_WIDTH = 2048
RET_WIDTH = 2048
SSM_GROUP = 16
SSM_STATE = 64
RET_HEAD_DIM = 256
RET_HEADS = 8
RET_CHUNK = 128
ROPE_BASE = 10000.0
EPS = 1e-6

V7X_LANES = 128
V7X_VMEM_BYTES = 64 * 1024 * 1024
V7X_COMPILER_SCRATCH_BYTES = 6 * 1024 * 1024

S5_CHUNK = 256
S5_POW_BLOCK = 16
S5_GROUPS_PER_TILE = V7X_LANES // SSM_GROUP
S5_TILE_STATES = S5_GROUPS_PER_TILE * SSM_STATE


def _nbytes(shape, dtype):
    return int(np.prod(shape)) * jnp.dtype(dtype).itemsize


def _params(semantics, pipelined_blocks, scratch_blocks=()):
    need = 2 * sum(_nbytes(s, d) for s, d in pipelined_blocks)
    need += sum(_nbytes(s, d) for s, d in scratch_blocks)
    limit = min(need + V7X_COMPILER_SCRATCH_BYTES, V7X_VMEM_BYTES - 4 * 1024 * 1024)
    return pltpu.CompilerParams(dimension_semantics=semantics, vmem_limit_bytes=limit)


def _ada_kernel(c_ref, w_ref, b_ref, o_ref):
    c = c_ref[...]
    c_act = c * jax.nn.sigmoid(c)
    o_ref[...] = jnp.dot(c_act.astype(BF16), w_ref[...].astype(BF16),
                         preferred_element_type=F32) + b_ref[...]


ADA_ROWS = 8
ADA_SIDE_K = 512
ADA_SIDE_N = 1024


def _ada_side(c_ref, w_ref, b_ref, mod_ref, step):
    kc, nc = w_ref.shape
    n_kb = c_ref.shape[1] // kc
    n_tiles = n_kb * (mod_ref.shape[1] // nc)

    @pl.when(step < n_tiles)
    def _():
        kb = step % n_kb
        c = c_ref[:, pl.ds(pl.multiple_of(kb * kc, kc), kc)]
        c_act = c * jax.nn.sigmoid(c)
        part = jnp.dot(c_act.astype(BF16), w_ref[...].astype(BF16), preferred_element_type=F32)
        cols = pl.ds(pl.multiple_of((step // n_kb) * nc, nc), nc)

        @pl.when(kb == 0)
        def _():
            mod_ref[:, cols] = part + b_ref[:, cols]

        @pl.when(kb > 0)
        def _():
            mod_ref[:, cols] += part


def _ada_mod(c_pad, w_ada, b_ada3, layer, *, tn=1024):
    _, d, n = w_ada.shape
    rows = ADA_ROWS
    blocks = [((rows, d), F32), ((d, tn), F32), ((1, tn), F32), ((rows, tn), F32)]
    return pl.pallas_call(
        _ada_kernel,
        grid=(n // tn,),
        in_specs=[pl.BlockSpec((rows, d), lambda j: (0, 0)),
                  pl.BlockSpec((None, d, tn), lambda j: (layer, 0, j)),
                  pl.BlockSpec((None, 1, tn), lambda j: (layer, 0, j))],
        out_specs=pl.BlockSpec((rows, tn), lambda j: (0, j)),
        out_shape=jax.ShapeDtypeStruct((rows, n), F32),
        compiler_params=_params(("parallel",), blocks),
        name="ada_mod",
    )(c_pad, w_ada, b_ada3)


def _normmod_kernel(x_ref, g_ref, sc_ref, sh_ref, o_ref):
    x = x_ref[...]
    ms = jnp.mean(x * x, axis=-1, keepdims=True)
    y = x * lax.rsqrt(ms + EPS) * g_ref[...]
    o_ref[...] = (y * (1.0 + sc_ref[...]) + sh_ref[...]).astype(o_ref.dtype)


def _normmod(x2, g, scale, shift, seq, *, tm=512):
    t, d = x2.shape
    bsz = scale.shape[0]
    blocks = [((tm, d), F32), ((1, d), F32), ((1, d), F32), ((1, d), F32), ((tm, d), BF16)]
    per_b = pl.BlockSpec((None, 1, d), lambda i: (i * tm // seq, 0, 0))
    return pl.pallas_call(
        _normmod_kernel,
        grid=(t // tm,),
        in_specs=[pl.BlockSpec((tm, d), lambda i: (i, 0)),
                  pl.BlockSpec((1, d), lambda i: (0, 0)), per_b, per_b],
        out_specs=pl.BlockSpec((tm, d), lambda i: (i, 0)),
        out_shape=jax.ShapeDtypeStruct((t, d), BF16),
        compiler_params=_params(("parallel",), blocks, [((tm, d), F32)]),
        name="normmod",
    )(x2, g.reshape(1, d), scale.reshape(bsz, 1, d), shift.reshape(bsz, 1, d))


def _rmsnorm_kernel(x_ref, g_ref, o_ref):
    x = x_ref[...]
    ms = jnp.mean(x * x, axis=-1, keepdims=True)
    o_ref[...] = x * lax.rsqrt(ms + EPS) * g_ref[...]


def _rmsnorm(x2, g, *, tm=512):
    t, d = x2.shape
    blocks = [((tm, d), F32), ((1, d), F32), ((tm, d), F32)]
    return pl.pallas_call(
        _rmsnorm_kernel,
        grid=(t // tm,),
        in_specs=[pl.BlockSpec((tm, d), lambda i: (i, 0)),
                  pl.BlockSpec((1, d), lambda i: (0, 0))],
        out_specs=pl.BlockSpec((tm, d), lambda i: (i, 0)),
        out_shape=jax.ShapeDtypeStruct((t, d), F32),
        compiler_params=_params(("parallel",), blocks, [((tm, d), F32)]),
        name="final_rmsnorm",
    )(x2, g.reshape(1, d))


def _epi_none(acc):
    return acc


def _epi_relu2(acc):
    r = jnp.maximum(acc, 0.0)
    return r * r


def _epi_resid(acc, x_ref, gate_ref):
    return x_ref[...] + gate_ref[...] * acc


def _mm_kernel(*refs, n_a, nk, epilogue, has_ada):
    a_refs, w_ref, rest = refs[:n_a], refs[n_a], refs[n_a + 1:]
    if has_ada:
        n_main = len(rest) - (5 if nk == 1 else 6)
        c_ref, wa_ref, ba_ref = rest[n_main:n_main + 3]
        mod_ref = rest[n_main + 4]
        step = ((pl.program_id(0) * pl.num_programs(1) + pl.program_id(1)) * nk
                + pl.program_id(2))
        _ada_side(c_ref, wa_ref, ba_ref, mod_ref, step)
        rest = rest[:n_main] + (rest[n_main + 3],) + rest[n_main + 5:]

    def product():
        total, row = None, 0
        for a_ref in a_refs:
            kk = a_ref.shape[1]
            part = jnp.dot(a_ref[...], w_ref[row:row + kk, :], preferred_element_type=F32)
            total = part if total is None else total + part
            row += kk
        return total

    if nk == 1:
        *extra, o_ref = rest
        o_ref[...] = epilogue(product(), *extra).astype(o_ref.dtype)
        return
    *extra, o_ref, acc_ref = rest
    k = pl.program_id(2)

    @pl.when(k == 0)
    def _():
        acc_ref[...] = product()

    @pl.when(jnp.logical_and(k > 0, k < nk - 1))
    def _():
        acc_ref[...] += product()

    @pl.when(k == nk - 1)
    def _():
        o_ref[...] = epilogue(acc_ref[...] + product(), *extra).astype(o_ref.dtype)


def _matmul(a_list, w, layer, *, tm, tn, tk, out_dtype, epilogue=_epi_none, resid=None,
            gate=None, seq=None, ada=None, name):
    m = a_list[0].shape[0]
    _, kdim, n = w.shape
    nk = kdim // tk
    if len(a_list) == 1:
        in_specs = [pl.BlockSpec((tm, tk), lambda i, j, k: (i, k))]
        blocks = [((tm, tk), a_list[0].dtype)]
    else:
        assert nk == 1 and sum(a.shape[1] for a in a_list) == kdim
        in_specs = [pl.BlockSpec((tm, a.shape[1]), lambda i, j, k: (i, 0)) for a in a_list]
        blocks = [((tm, a.shape[1]), a.dtype) for a in a_list]
    in_specs.append(pl.BlockSpec((None, tk, tn), lambda i, j, k: (layer, k, j)))
    blocks += [((tk, tn), w.dtype), ((tm, tn), out_dtype)]
    args = list(a_list) + [w]
    if resid is not None:
        bsz = gate.shape[0]
        in_specs += [pl.BlockSpec((tm, tn), lambda i, j, k: (i, j)),
                     pl.BlockSpec((None, 1, tn), lambda i, j, k: (i * tm // seq, 0, j))]
        blocks += [((tm, tn), F32), ((1, tn), F32)]
        args += [resid, gate.reshape(bsz, 1, n)]
    scratch = [] if nk == 1 else [pltpu.VMEM((tm, tn), F32)]
    scratch_blocks = [((tm, tn), F32)] * (1 if nk == 1 else 2)
    out_specs = pl.BlockSpec((tm, tn), lambda i, j, k: (i, j))
    out_shape = jax.ShapeDtypeStruct((m, n), out_dtype)
    if ada is not None:
        c_pad, w_ada, b_ada3, ada_layer = ada
        _, d_ada, n_ada = w_ada.shape
        n_kb, n_cb = d_ada // ADA_SIDE_K, n_ada // ADA_SIDE_N
        nj = n // tn
        assert n_kb * n_cb <= (m // tm) * nj * nk
        tile = lambda i, j, k: jnp.minimum((i * nj + j) * nk + k, n_kb * n_cb - 1)
        in_specs += [pl.BlockSpec((ADA_ROWS, d_ada), lambda i, j, k: (0, 0)),
                     pl.BlockSpec((None, ADA_SIDE_K, ADA_SIDE_N),
                                  lambda i, j, k: (ada_layer, tile(i, j, k) % n_kb, tile(i, j, k) // n_kb)),
                     pl.BlockSpec((None, 1, n_ada), lambda i, j, k: (ada_layer, 0, 0))]
        out_specs = [out_specs, pl.BlockSpec((ADA_ROWS, n_ada), lambda i, j, k: (0, 0))]
        out_shape = [out_shape, jax.ShapeDtypeStruct((ADA_ROWS, n_ada), F32)]
        blocks += [((ADA_ROWS, d_ada), F32), ((ADA_SIDE_K, ADA_SIDE_N), F32), ((1, n_ada), F32),
                   ((ADA_ROWS, n_ada), F32)]
        args += [c_pad, w_ada, b_ada3]
    return pl.pallas_call(
        functools.partial(_mm_kernel, n_a=len(a_list), nk=nk, epilogue=epilogue,
                          has_ada=ada is not None),
        grid=(m // tm, n // tn, nk),
        in_specs=in_specs,
        out_specs=out_specs,
        out_shape=out_shape,
        scratch_shapes=scratch,
        compiler_params=_params(("arbitrary",) * 3 if ada is not None
                                else ("parallel", "parallel", "arbitrary"), blocks, scratch_blocks),
        name=name,
    )(*args)


def _mm_cast_kernel(a_ref, wc_ref, *rest, epilogue, has_side):
    if has_side:
        side_ref, o_ref, side_o_ref, wb_ref = rest
        side_o_ref[...] = side_ref[...].astype(BF16)
    else:
        o_ref, wb_ref = rest
    j = pl.program_id(0)
    i = pl.program_id(1)
    nj = pl.num_programs(0) - 1
    rows = wc_ref.shape[0]

    @pl.when(j < nj)
    def _():
        wb_ref[j % 2, pl.ds(pl.multiple_of(i * rows, rows), rows), :] = wc_ref[...].astype(BF16)

    @pl.when(j > 0)
    def _():
        prod = jnp.dot(a_ref[...], wb_ref[(j - 1) % 2], preferred_element_type=F32)
        o_ref[...] = epilogue(prod).astype(o_ref.dtype)


def _matmul_cast(a, w, layer, *, tm, tn, out_dtype, epilogue=_epi_none, side=None,
                 side_rows=None, name):
    m, kdim = a.shape
    n = w.shape[2]
    ni, nj = m // tm, n // tn
    rows = kdim // ni
    row_blk = lambda j, i: jnp.where(j > 0, i, 0)
    col_blk = lambda j: jnp.maximum(j - 1, 0)
    blocks = [((tm, kdim), a.dtype), ((rows, tn), w.dtype), ((tm, tn), out_dtype)]
    scratch_blocks = [((2, kdim, tn), BF16), ((tm, tn), F32)]
    in_specs = [pl.BlockSpec((tm, kdim), lambda j, i: (row_blk(j, i), 0)),
                pl.BlockSpec((None, rows, tn),
                             lambda j, i: (layer, jnp.where(j < nj, i, ni - 1),
                                           jnp.minimum(j, nj - 1)))]
    out_specs = pl.BlockSpec((tm, tn), lambda j, i: (row_blk(j, i), col_blk(j)))
    out_shape = jax.ShapeDtypeStruct((m, n), out_dtype)
    args = [a, w]
    if side is not None:
        _, r, c = side.shape
        n_side = r // side_rows
        assert n_side * side_rows == r and n_side <= (nj + 1) * ni
        side_blk = lambda j, i: jnp.minimum(j * ni + i, n_side - 1)
        in_specs.append(pl.BlockSpec((None, side_rows, c), lambda j, i: (layer, side_blk(j, i), 0)))
        out_specs = [out_specs, pl.BlockSpec((side_rows, c), lambda j, i: (side_blk(j, i), 0))]
        out_shape = [out_shape, jax.ShapeDtypeStruct((r, c), BF16)]
        blocks += [((side_rows, c), F32), ((side_rows, c), BF16)]
        args.append(side)
    return pl.pallas_call(
        functools.partial(_mm_cast_kernel, epilogue=epilogue, has_side=side is not None),
        grid=(nj + 1, ni),
        in_specs=in_specs,
        out_specs=out_specs,
        out_shape=out_shape,
        scratch_shapes=[pltpu.VMEM((2, kdim, tn), BF16)],
        compiler_params=_params(("arbitrary", "arbitrary"), blocks, scratch_blocks),
        name=name,
    )(*args)


def _rope_kernel(pos_ref, inv_ref, cos_ref, sin_ref):
    ang = pos_ref[...].astype(F32) * inv_ref[...]
    cos_ref[...] = jnp.cos(ang)
    sin_ref[...] = jnp.sin(ang)


def _rope_tables(positions, *, tm=1024):
    t = positions.size
    half = RET_HEAD_DIM // 2
    inv = ROPE_BASE ** (-np.arange(0, RET_HEAD_DIM, 2, dtype=np.float64) / RET_HEAD_DIM)
    inv = jnp.asarray(inv.astype(np.float32)).reshape(1, half)
    blocks = [((tm, 1), jnp.int32), ((1, half), F32), ((tm, half), F32), ((tm, half), F32)]
    return pl.pallas_call(
        _rope_kernel,
        grid=(t // tm,),
        in_specs=[pl.BlockSpec((tm, 1), lambda i: (i, 0)),
                  pl.BlockSpec((1, half), lambda i: (0, 0))],
        out_specs=[pl.BlockSpec((tm, half), lambda i: (i, 0))] * 2,
        out_shape=[jax.ShapeDtypeStruct((t, half), F32)] * 2,
        compiler_params=_params(("parallel",), blocks),
        name="rope_tables",
    )(positions.reshape(t, 1), inv)


def _s5_prep_kernel(lr_ref, li_ref, ldt_ref, br_ref, bi_ref, cr_ref, ci_ref,
                    bblk_ref, cblk_ref, pn_ref, pp_ref):
    w = S5_TILE_STATES
    lr = lr_ref[...]
    li = li_ref[...]
    dt = jnp.exp(ldt_ref[...])
    a = lr * dt
    th = li * dt
    mag = jnp.exp(a)
    lbr = mag * jnp.cos(th)
    lbi = mag * jnp.sin(th)
    den = lr * lr + li * li
    nr = lbr - 1.0
    coef_r = (nr * lr + lbi * li) / den
    coef_i = (lbi * lr - nr * li) / den
    br = br_ref[...]
    bi = bi_ref[...]
    bbar_r = coef_r * br - coef_i * bi
    bbar_i = coef_r * bi + coef_i * br

    rows = lax.broadcasted_iota(jnp.int32, (V7X_LANES, w), 0)
    cols = lax.broadcasted_iota(jnp.int32, (V7X_LANES, w), 1)
    same_group = ((rows >> (SSM_GROUP.bit_length() - 1))
                  == (cols >> (SSM_STATE.bit_length() - 1)))
    reps = (S5_GROUPS_PER_TILE, 1)

    def blockdiag(v):
        return jnp.where(same_group, jnp.tile(v, reps), 0.0)

    bblk_ref[:, :w] = blockdiag(bbar_r).astype(bblk_ref.dtype)
    bblk_ref[:, w:] = blockdiag(bbar_i).astype(bblk_ref.dtype)
    cblk_ref[:, :w] = blockdiag(cr_ref[...]).astype(cblk_ref.dtype)
    cblk_ref[:, w:] = blockdiag(-ci_ref[...]).astype(cblk_ref.dtype)

    blk = S5_POW_BLOCK
    lo_m = (lax.broadcasted_iota(jnp.int32, (blk, w), 0) + 1).astype(F32)
    hi_m = (lax.broadcasted_iota(jnp.int32, (S5_CHUNK // blk, w), 0) * blk).astype(F32)

    def powers(m, sign):
        mag_m = jnp.exp(sign * (m * a))
        return mag_m * jnp.cos(m * th), sign * (mag_m * jnp.sin(m * th))

    for sign, ref in ((1.0, pp_ref), (-1.0, pn_ref)):
        lo_r, lo_i = powers(lo_m, sign)
        hi_r, hi_i = powers(hi_m, sign)
        for q in range(S5_CHUNK // blk):
            h_re, h_im = hi_r[q:q + 1, :], hi_i[q:q + 1, :]
            ref[q * blk:(q + 1) * blk, :w] = h_re * lo_r - h_im * lo_i
            ref[q * blk:(q + 1) * blk, w:] = h_re * lo_i + h_im * lo_r


def _s5_prep(lam_re, lam_im, log_dt, b_re, b_im, c_re, c_im):
    depth, g, p, h = b_re.shape
    gl, w = S5_GROUPS_PER_TILE, S5_TILE_STATES
    nt = depth * g // gl
    row = lambda v: v.reshape(nt, 1, w)
    bt = lambda v: v.reshape(nt, gl, p, h).transpose(0, 3, 1, 2).reshape(nt, h, w)
    ct = lambda v: v.reshape(nt, gl, h, p).transpose(0, 2, 1, 3).reshape(nt, h, w)
    row_spec = pl.BlockSpec((None, 1, w), lambda j: (j, 0, 0))
    mat_spec = pl.BlockSpec((None, h, w), lambda j: (j, 0, 0))
    blk_spec = pl.BlockSpec((None, V7X_LANES, 2 * w), lambda j: (j, 0, 0))
    pow_spec = pl.BlockSpec((None, S5_CHUNK, 2 * w), lambda j: (j, 0, 0))
    blocks = ([((1, w), F32)] * 3 + [((h, w), F32)] * 4 + [((V7X_LANES, 2 * w), BF16)] * 2
              + [((S5_CHUNK, 2 * w), F32)] * 2)
    return pl.pallas_call(
        _s5_prep_kernel,
        grid=(nt,),
        in_specs=[row_spec] * 3 + [mat_spec] * 4,
        out_specs=[blk_spec, blk_spec, pow_spec, pow_spec],
        out_shape=[jax.ShapeDtypeStruct((nt, V7X_LANES, 2 * w), BF16)] * 2
        + [jax.ShapeDtypeStruct((nt, S5_CHUNK, 2 * w), F32)] * 2,
        compiler_params=_params(("parallel",), blocks),
        name="s5_prep",
    )(row(lam_re), row(lam_im), row(jnp.repeat(log_dt, p)), bt(b_re), bt(b_im), ct(c_re), ct(c_im))


def _s5_kernel(u_ref, bblk_ref, cblk_ref, pn_ref, pp_ref, d_ref, y_ref, *, n_chunks):
    w = S5_TILE_STATES
    lc = S5_CHUNK
    bblk = bblk_ref[...]
    cblk = cblk_ref[...]
    d = d_ref[...]
    r = lax.broadcasted_iota(jnp.int32, (lc, lc), 0)
    c = lax.broadcasted_iota(jnp.int32, (lc, lc), 1)
    tri = (r >= c).astype(BF16)

    def chunk(t, carry):
        cr, ci = carry
        rows = pl.ds(pl.multiple_of(t * lc, lc), lc)
        u = u_ref[rows, :]
        bu = jnp.dot(u.astype(BF16), bblk, preferred_element_type=F32)
        br, bi = bu[:, :w], bu[:, w:]
        pnr, pni = pn_ref[:, :w], pn_ref[:, w:]
        z = jnp.concatenate([pnr * br - pni * bi, pnr * bi + pni * br], axis=1)
        ws = jnp.dot(tri, z.astype(BF16), preferred_element_type=F32)
        wr = ws[:, :w] + cr
        wi = ws[:, w:] + ci
        ppr, ppi = pp_ref[:, :w], pp_ref[:, w:]
        xr = ppr * wr - ppi * wi
        xi = ppr * wi + ppi * wr
        x = jnp.concatenate([xr, xi], axis=1).astype(BF16)
        y = lax.dot_general(x, cblk, (((1,), (1,)), ((), ())), preferred_element_type=F32)
        y_ref[rows, :] = jax.nn.gelu(y + d * u)
        return xr[lc - 1:, :], xi[lc - 1:, :]

    zero = jnp.zeros((1, w), F32)
    lax.fori_loop(0, n_chunks, chunk, (zero, zero), unroll=16)


def _s5_scan(proj, tables, layer, d_skip, bsz, seq):
    bblk, cblk, pn, pp = tables
    nt = SSM_WIDTH // V7X_LANES
    w2 = 2 * S5_TILE_STATES
    tab = lambda rows: pl.BlockSpec((None, rows, w2), lambda j, b: (layer * nt + j, 0, 0))
    blocks = [((seq, V7X_LANES), F32), ((V7X_LANES, w2), BF16), ((V7X_LANES, w2), BF16),
              ((S5_CHUNK, w2), F32), ((S5_CHUNK, w2), F32), ((1, V7X_LANES), F32),
              ((seq, V7X_LANES), F32)]
    return pl.pallas_call(
        functools.partial(_s5_kernel, n_chunks=seq // S5_CHUNK),
        grid=(nt, bsz),
        in_specs=[pl.BlockSpec((seq, V7X_LANES), lambda j, b: (b, j)),
                  tab(V7X_LANES), tab(V7X_LANES), tab(S5_CHUNK), tab(S5_CHUNK),
                  pl.BlockSpec((1, V7X_LANES), lambda j, b: (0, j))],
        out_specs=pl.BlockSpec((seq, V7X_LANES), lambda j, b: (b, j)),
        out_shape=jax.ShapeDtypeStruct((bsz * seq, SSM_WIDTH), F32),
        compiler_params=_params(("parallel", "parallel"), blocks),
        name="s5_scan",
    )(proj, bblk, cblk, pn, pp, d_skip.reshape(1, SSM_WIDTH))


def _glu_kernel(y_ref, w_ref, b_ref, g_ref, o_ref):
    y = y_ref[...]
    z = jnp.dot(y.astype(BF16), w_ref[...], preferred_element_type=F32) + b_ref[...]
    v = y * jax.nn.sigmoid(z)
    ms = jnp.mean(v * v, axis=-1, keepdims=True)
    o_ref[...] = (v * lax.rsqrt(ms + EPS) * g_ref[...]).astype(o_ref.dtype)


def _glu_norm(y, w_glu, layer, b_glu, g_out, *, tm=512):
    t, n = y.shape
    blocks = [((tm, n), F32), ((n, n), BF16), ((1, n), F32), ((1, n), F32), ((tm, n), BF16)]
    vec = pl.BlockSpec((1, n), lambda i: (0, 0))
    return pl.pallas_call(
        _glu_kernel,
        grid=(t // tm,),
        in_specs=[pl.BlockSpec((tm, n), lambda i: (i, 0)),
                  pl.BlockSpec((None, n, n), lambda i: (layer, 0, 0)), vec, vec],
        out_specs=pl.BlockSpec((tm, n), lambda i: (i, 0)),
        out_shape=jax.ShapeDtypeStruct((t, n), BF16),
        compiler_params=_params(("parallel",), blocks),
        name="glu_norm",
    )(y, w_glu, b_glu.reshape(1, n), g_out.reshape(1, n))


def _retention_tables():
    h, c, dh = RET_HEADS, RET_CHUNK, RET_HEAD_DIM
    log_gamma = np.log1p(-np.exp2(-5.0 - np.arange(h, dtype=np.float64)))
    idx = np.arange(c, dtype=np.float64)
    rel = idx[:, None] - idx[None, :]
    mask = np.where(rel >= 0, np.exp(log_gamma[:, None, None] * np.maximum(rel, 0.0)), 0.0)
    cross = np.exp(log_gamma[:, None] * (idx + 1.0))
    state = np.exp(log_gamma[:, None] * (c - 1.0 - idx))
    chunk = np.exp(log_gamma * c)
    k_scale = dh ** -0.5
    assert np.log2(k_scale) == np.round(np.log2(k_scale))
    mask = mask * k_scale
    state = state * k_scale
    wide = lambda v: np.broadcast_to(v[:, :, None], (h, c, dh))
    f = lambda v: jnp.asarray(np.ascontiguousarray(v).astype(np.float32))
    return f(mask), f(wide(cross)), f(wide(state)), f(np.broadcast_to(chunk[:, None, None], (h, 1, dh)))


def _retention_kernel(q_ref, k_ref, v_ref, g_ref, cos_ref, sin_ref, mask_ref, cd_ref, sd_ref,
                      ch_ref, gn_ref, o_ref, state_ref, *, chunks_per_step, heads_per_step):
    half = RET_HEAD_DIM // 2
    dh = RET_HEAD_DIM
    c = RET_CHUNK
    head0 = pl.program_id(2) * heads_per_step

    @pl.when(pl.program_id(1) == 0)
    def _():
        for e in range(heads_per_step):
            state_ref[head0 + e] = jnp.zeros(state_ref.shape[1:], F32)

    def rotate(t, cos, sin):
        t1, t2 = t[:, :half], t[:, half:]
        return jnp.concatenate([t1 * cos - t2 * sin, t1 * sin + t2 * cos], axis=1)

    for ci in range(chunks_per_step):
        rows = pl.ds(ci * c, c)
        cos = cos_ref[rows, :]
        sin = sin_ref[rows, :]
        for e in range(heads_per_step):
            hh = head0 + e
            cols = slice(e * dh, (e + 1) * dh)
            q = rotate(q_ref[rows, cols], cos, sin)
            k = rotate(k_ref[rows, cols], cos, sin)
            v = v_ref[rows, cols].astype(BF16)
            qb = q.astype(BF16)
            state = state_ref[hh]
            scores = lax.dot_general(qb, k.astype(BF16), (((1,), (1,)), ((), ())),
                                     preferred_element_type=F32) * mask_ref[hh]
            inner = jnp.dot(scores.astype(BF16), v, preferred_element_type=F32)
            cross = jnp.dot(qb, state.astype(BF16), preferred_element_type=F32) * cd_ref[hh]
            kv = lax.dot_general((k * sd_ref[hh]).astype(BF16), v, (((0,), (0,)), ((), ())),
                                 preferred_element_type=F32)
            state_ref[hh] = state * ch_ref[hh] + kv
            out = inner + cross
            mu = jnp.mean(out, axis=-1, keepdims=True)
            dev = out - mu
            var = jnp.mean(dev * dev, axis=-1, keepdims=True)
            g = g_ref[rows, cols]
            o_ref[rows, cols] = (dev * lax.rsqrt(var + EPS) * gn_ref[hh]
                                 * (g * jax.nn.sigmoid(g))).astype(o_ref.dtype)


def _retention(proj, cos, sin, g_norm, bsz, seq, *, chunks_per_step=8, heads_per_step=4):
    dh, h = RET_HEAD_DIM, RET_HEADS
    rb = chunks_per_step * RET_CHUNK
    steps = seq // rb
    wide = heads_per_step * dh
    groups = h // heads_per_step
    mask, cd, sd, ch = _retention_tables()
    first = SSM_WIDTH // wide
    col = lambda which: pl.BlockSpec(
        (rb, wide), lambda b, s, hg, which=which: (b * steps + s, first + which * groups + hg))
    rope = pl.BlockSpec((rb, dh // 2), lambda b, s, hg: (b * steps + s, 0))
    whole = lambda rows, cols: pl.BlockSpec((h, rows, cols), lambda b, s, hg: (0, 0, 0))
    blocks = ([((rb, wide), F32)] * 4 + [((rb, dh // 2), F32)] * 2
              + [((h, RET_CHUNK, RET_CHUNK), F32), ((h, RET_CHUNK, dh), F32),
                 ((h, RET_CHUNK, dh), F32), ((h, 1, dh), F32), ((h, 1, dh), F32), ((rb, wide), BF16)])
    return pl.pallas_call(
        functools.partial(_retention_kernel, chunks_per_step=chunks_per_step,
                          heads_per_step=heads_per_step),
        grid=(bsz, steps, groups),
        in_specs=[col(0), col(1), col(2), col(3), rope, rope,
                  whole(RET_CHUNK, RET_CHUNK), whole(RET_CHUNK, dh), whole(RET_CHUNK, dh),
                  whole(1, dh), whole(1, dh)],
        out_specs=pl.BlockSpec((rb, wide), lambda b, s, hg: (b * steps + s, hg)),
        out_shape=jax.ShapeDtypeStruct((bsz * seq, RET_WIDTH), BF16),
        scratch_shapes=[pltpu.VMEM((h, dh, dh), F32)],
        compiler_params=_params(("parallel", "arbitrary", "arbitrary"), blocks,
                                [((h, dh, dh), F32)]),
        name="retention",
    )(proj, proj, proj, proj, cos, sin, mask, cd, sd, ch, g_norm.reshape(h, 1, dh))


def kernel(x, c, positions, w_ada, b_ada, g_mix, w_in, lam_re, lam_im, log_dt, b_re, b_im, c_re,
           c_im, d_skip, w_glu, b_glu, g_ssm_out, g_ret_norm, w_out, g_mlp, w_up, w_down, g_final):
    bsz, seq, d = x.shape
    depth = w_ada.shape[0]
    x2 = x.reshape(bsz * seq, d)
    c_pad = jnp.zeros((ADA_ROWS, d), F32).at[:bsz].set(c)
    b_ada3 = b_ada.reshape(depth, 1, b_ada.shape[1])
    mod = _ada_mod(c_pad, w_ada, b_ada3, 0)
    cos, sin = _rope_tables(positions)
    tables = _s5_prep(lam_re, lam_im, log_dt, b_re, b_im, c_re, c_im)
    w_glu_b = w_glu.astype(BF16)
    for i in range(depth):
        shift1, scale1, gate1, shift2, scale2, gate2 = jnp.split(mod[:bsz], 6, axis=-1)

        h = _normmod(x2, g_mix[i], scale1, shift1, seq)
        proj, w_out_b = _matmul_cast(h, w_in, i, tm=1024, tn=1024, out_dtype=F32,
                                     side=w_out, side_rows=64, name="in_proj")
        y = _s5_scan(proj, tables, i, d_skip[i], bsz, seq)
        ssm_out = _glu_norm(y, w_glu_b, i, b_glu[i], g_ssm_out[i])
        ret_out = _retention(proj, cos, sin, g_ret_norm[i], bsz, seq)
        x2 = _matmul([ssm_out, ret_out], w_out_b[None], 0, tm=1024, tn=1024, tk=d, out_dtype=F32,
                     epilogue=_epi_resid, resid=x2, gate=gate1, seq=seq, name="out_proj")

        h = _normmod(x2, g_mlp[i], scale2, shift2, seq)
        a, w_down_b = _matmul_cast(h, w_up, i, tm=1024, tn=1024, out_dtype=BF16,
                                   epilogue=_epi_relu2, side=w_down, side_rows=128, name="mlp_up")
        ada = (c_pad, w_ada, b_ada3, i + 1) if i + 1 < depth else None
        res = _matmul([a], w_down_b[None], 0, tm=1024, tn=1024, tk=2048, out_dtype=F32,
                      epilogue=_epi_resid, resid=x2, gate=gate2, seq=seq, ada=ada, name="mlp_down")
        x2, mod = res if ada is not None else (res, None)
    return _rmsnorm(x2, g_final).reshape(bsz, seq, d)
```

```python
import functools

import numpy as np
import jax
import jax.numpy as jnp
from jax import lax
from jax.experimental import pallas as pl
from jax.experimental.pallas import tpu as pltpu

F32 = jnp.float32
BF16 = jnp.bfloat16

SSM_WIDTH = 2048
RET_WIDTH = 2048
SSM_GROUP = 16
SSM_STATE = 64
RET_HEAD_DIM = 256
RET_HEADS = 8
RET_CHUNK = 128
ROPE_BASE = 10000.0
EPS = 1e-6

V7X_LANES = 128
V7X_VMEM_BYTES = 64 * 1024 * 1024
V7X_COMPILER_SCRATCH_BYTES = 6 * 1024 * 1024

S5_CHUNK = 256
S5_POW_BLOCK = 16
S5_GROUPS_PER_TILE = V7X_LANES // SSM_GROUP
S5_TILE_STATES = S5_GROUPS_PER_TILE * SSM_STATE


def _nbytes(shape, dtype):
    return int(np.prod(shape)) * jnp.dtype(dtype).itemsize


def _params(semantics, pipelined_blocks, scratch_blocks=()):
    need = 2 * sum(_nbytes(s, d) for s, d in pipelined_blocks)
    need += sum(_nbytes(s, d) for s, d in scratch_blocks)
    limit = min(need + V7X_COMPILER_SCRATCH_BYTES, V7X_VMEM_BYTES - 4 * 1024 * 1024)
    return pltpu.CompilerParams(dimension_semantics=semantics, vmem_limit_bytes=limit)


def _ada_kernel(c_ref, w_ref, b_ref, o_ref):
    c = c_ref[...]
    c_act = c * jax.nn.sigmoid(c)
    o_ref[...] = jnp.dot(c_act.astype(BF16), w_ref[...].astype(BF16),
                         preferred_element_type=F32) + b_ref[...]


ADA_ROWS = 8
ADA_SIDE_K = 512
ADA_SIDE_N = 1024


def _ada_side(c_ref, w_ref, b_ref, mod_ref, step):
    kc, nc = w_ref.shape
    n_kb = c_ref.shape[1] // kc
    n_tiles = n_kb * (mod_ref.shape[1] // nc)

    @pl.when(step < n_tiles)
    def _():
        kb = step % n_kb
        c = c_ref[:, pl.ds(pl.multiple_of(kb * kc, kc), kc)]
        c_act = c * jax.nn.sigmoid(c)
        part = jnp.dot(c_act.astype(BF16), w_ref[...].astype(BF16), preferred_element_type=F32)
        cols = pl.ds(pl.multiple_of((step // n_kb) * nc, nc), nc)

        @pl.when(kb == 0)
        def _():
            mod_ref[:, cols] = part + b_ref[:, cols]

        @pl.when(kb > 0)
        def _():
            mod_ref[:, cols] += part


def _ada_mod(c_pad, w_ada, b_ada3, layer, *, tn=1024):
    _, d, n = w_ada.shape
    rows = ADA_ROWS
    blocks = [((rows, d), F32), ((d, tn), F32), ((1, tn), F32), ((rows, tn), F32)]
    return pl.pallas_call(
        _ada_kernel,
        grid=(n // tn,),
        in_specs=[pl.BlockSpec((rows, d), lambda j: (0, 0)),
                  pl.BlockSpec((None, d, tn), lambda j: (layer, 0, j)),
                  pl.BlockSpec((None, 1, tn), lambda j: (layer, 0, j))],
        out_specs=pl.BlockSpec((rows, tn), lambda j: (0, j)),
        out_shape=jax.ShapeDtypeStruct((rows, n), F32),
        compiler_params=_params(("parallel",), blocks),
        name="ada_mod",
    )(c_pad, w_ada, b_ada3)


def _normmod_kernel(x_ref, g_ref, sc_ref, sh_ref, o_ref):
    x = x_ref[...].astype(F32)
    ms = jnp.mean(x * x, axis=-1, keepdims=True)
    y = x * lax.rsqrt(ms + EPS) * g_ref[...]
    o_ref[...] = (y * (1.0 + sc_ref[...]) + sh_ref[...]).astype(o_ref.dtype)


def _normmod(x2, g, scale, shift, seq, *, tm=512):
    t, d = x2.shape
    bsz = scale.shape[0]
    blocks = [((tm, d), x2.dtype), ((1, d), F32), ((1, d), F32), ((1, d), F32), ((tm, d), BF16)]
    per_b = pl.BlockSpec((None, 1, d), lambda i: (i * tm // seq, 0, 0))
    return pl.pallas_call(
        _normmod_kernel,
        grid=(t // tm,),
        in_specs=[pl.BlockSpec((tm, d), lambda i: (i, 0)),
                  pl.BlockSpec((1, d), lambda i: (0, 0)), per_b, per_b],
        out_specs=pl.BlockSpec((tm, d), lambda i: (i, 0)),
        out_shape=jax.ShapeDtypeStruct((t, d), BF16),
        compiler_params=_params(("parallel",), blocks, [((tm, d), F32)]),
        name="normmod",
    )(x2, g.reshape(1, d), scale.reshape(bsz, 1, d), shift.reshape(bsz, 1, d))


def _rmsnorm_kernel(x_ref, g_ref, o_ref):
    x = x_ref[...]
    ms = jnp.mean(x * x, axis=-1, keepdims=True)
    o_ref[...] = x * lax.rsqrt(ms + EPS) * g_ref[...]


def _rmsnorm(x2, g, *, tm=512):
    t, d = x2.shape
    blocks = [((tm, d), F32), ((1, d), F32), ((tm, d), F32)]
    return pl.pallas_call(
        _rmsnorm_kernel,
        grid=(t // tm,),
        in_specs=[pl.BlockSpec((tm, d), lambda i: (i, 0)),
                  pl.BlockSpec((1, d), lambda i: (0, 0))],
        out_specs=pl.BlockSpec((tm, d), lambda i: (i, 0)),
        out_shape=jax.ShapeDtypeStruct((t, d), F32),
        compiler_params=_params(("parallel",), blocks, [((tm, d), F32)]),
        name="final_rmsnorm",
    )(x2, g.reshape(1, d))


def _epi_none(acc):
    return acc


def _epi_relu2(acc):
    r = jnp.maximum(acc, 0.0)
    return r * r


def _epi_resid(acc, x_ref, gate_ref):
    return x_ref[...] + gate_ref[...] * acc


def _mm_kernel(*refs, n_a, nk, epilogue, has_ada, has_copy):
    a_refs, w_ref, rest = refs[:n_a], refs[n_a], refs[n_a + 1:]
    copy_ref = None
    if has_copy:
        at = len(rest) - (1 if nk == 1 else 2)
        copy_ref = rest[at]
        rest = rest[:at] + rest[at + 1:]
    if has_ada:
        n_main = len(rest) - (5 if nk == 1 else 6)
        c_ref, wa_ref, ba_ref = rest[n_main:n_main + 3]
        mod_ref = rest[n_main + 4]
        step = ((pl.program_id(0) * pl.num_programs(1) + pl.program_id(1)) * nk
                + pl.program_id(2))
        _ada_side(c_ref, wa_ref, ba_ref, mod_ref, step)
        rest = rest[:n_main] + (rest[n_main + 3],) + rest[n_main + 5:]

    def product():
        total, row = None, 0
        for a_ref in a_refs:
            kk = a_ref.shape[1]
            part = jnp.dot(a_ref[...], w_ref[row:row + kk, :], preferred_element_type=F32)
            total = part if total is None else total + part
            row += kk
        return total

    def write(o_ref, val):
        o_ref[...] = val.astype(o_ref.dtype)
        if copy_ref is not None:
            copy_ref[...] = val.astype(copy_ref.dtype)

    if nk == 1:
        *extra, o_ref = rest
        write(o_ref, epilogue(product(), *extra))
        return
    *extra, o_ref, acc_ref = rest
    k = pl.program_id(2)

    @pl.when(k == 0)
    def _():
        acc_ref[...] = product()

    @pl.when(jnp.logical_and(k > 0, k < nk - 1))
    def _():
        acc_ref[...] += product()

    @pl.when(k == nk - 1)
    def _():
        write(o_ref, epilogue(acc_ref[...] + product(), *extra))


def _matmul(a_list, w, layer, *, tm, tn, tk, out_dtype, epilogue=_epi_none, resid=None,
            gate=None, seq=None, ada=None, copy_dtype=None, name):
    m = a_list[0].shape[0]
    _, kdim, n = w.shape
    nk = kdim // tk
    if len(a_list) == 1:
        in_specs = [pl.BlockSpec((tm, tk), lambda i, j, k: (i, k))]
        blocks = [((tm, tk), a_list[0].dtype)]
    else:
        assert nk == 1 and sum(a.shape[1] for a in a_list) == kdim
        in_specs = [pl.BlockSpec((tm, a.shape[1]), lambda i, j, k: (i, 0)) for a in a_list]
        blocks = [((tm, a.shape[1]), a.dtype) for a in a_list]
    in_specs.append(pl.BlockSpec((None, tk, tn), lambda i, j, k: (layer, k, j)))
    blocks += [((tk, tn), w.dtype), ((tm, tn), out_dtype)]
    args = list(a_list) + [w]
    if resid is not None:
        bsz = gate.shape[0]
        in_specs += [pl.BlockSpec((tm, tn), lambda i, j, k: (i, j)),
                     pl.BlockSpec((None, 1, tn), lambda i, j, k: (i * tm // seq, 0, j))]
        blocks += [((tm, tn), F32), ((1, tn), F32)]
        args += [resid, gate.reshape(bsz, 1, n)]
    scratch = [] if nk == 1 else [pltpu.VMEM((tm, tn), F32)]
    scratch_blocks = [((tm, tn), F32)] * (1 if nk == 1 else 2)
    out_specs = pl.BlockSpec((tm, tn), lambda i, j, k: (i, j))
    out_shape = jax.ShapeDtypeStruct((m, n), out_dtype)
    if ada is not None:
        c_pad, w_ada, b_ada3, ada_layer = ada
        _, d_ada, n_ada = w_ada.shape
        n_kb, n_cb = d_ada // ADA_SIDE_K, n_ada // ADA_SIDE_N
        nj = n // tn
        assert n_kb * n_cb <= (m // tm) * nj * nk
        tile = lambda i, j, k: jnp.minimum((i * nj + j) * nk + k, n_kb * n_cb - 1)
        in_specs += [pl.BlockSpec((ADA_ROWS, d_ada), lambda i, j, k: (0, 0)),
                     pl.BlockSpec((None, ADA_SIDE_K, ADA_SIDE_N),
                                  lambda i, j, k: (ada_layer, tile(i, j, k) % n_kb, tile(i, j, k) // n_kb)),
                     pl.BlockSpec((None, 1, n_ada), lambda i, j, k: (ada_layer, 0, 0))]
        out_specs = [out_specs, pl.BlockSpec((ADA_ROWS, n_ada), lambda i, j, k: (0, 0))]
        out_shape = [out_shape, jax.ShapeDtypeStruct((ADA_ROWS, n_ada), F32)]
        blocks += [((ADA_ROWS, d_ada), F32), ((ADA_SIDE_K, ADA_SIDE_N), F32), ((1, n_ada), F32),
                   ((ADA_ROWS, n_ada), F32)]
        args += [c_pad, w_ada, b_ada3]
    if copy_dtype is not None:
        out_specs = (out_specs if isinstance(out_specs, list) else [out_specs]) + [
            pl.BlockSpec((tm, tn), lambda i, j, k: (i, j))]
        out_shape = (out_shape if isinstance(out_shape, list) else [out_shape]) + [
            jax.ShapeDtypeStruct((m, n), copy_dtype)]
        blocks.append(((tm, tn), copy_dtype))
    return pl.pallas_call(
        functools.partial(_mm_kernel, n_a=len(a_list), nk=nk, epilogue=epilogue,
                          has_ada=ada is not None, has_copy=copy_dtype is not None),
        grid=(m // tm, n // tn, nk),
        in_specs=in_specs,
        out_specs=out_specs,
        out_shape=out_shape,
        scratch_shapes=scratch,
        compiler_params=_params(("arbitrary",) * 3 if ada is not None
                                else ("parallel", "parallel", "arbitrary"), blocks, scratch_blocks),
        name=name,
    )(*args)


def _mm_cast_kernel(a_ref, wc_ref, *rest, epilogue, has_side):
    if has_side:
        side_ref, o_ref, side_o_ref, wb_ref = rest
        side_o_ref[...] = side_ref[...].astype(BF16)
    else:
        o_ref, wb_ref = rest
    j = pl.program_id(0)
    i = pl.program_id(1)
    nj = pl.num_programs(0) - 1
    rows = wc_ref.shape[0]

    @pl.when(j < nj)
    def _():
        wb_ref[j % 2, pl.ds(pl.multiple_of(i * rows, rows), rows), :] = wc_ref[...].astype(BF16)

    @pl.when(j > 0)
    def _():
        prod = jnp.dot(a_ref[...], wb_ref[(j - 1) % 2], preferred_element_type=F32)
        o_ref[...] = epilogue(prod).astype(o_ref.dtype)


def _matmul_cast(a, w, layer, *, tm, tn, out_dtype, epilogue=_epi_none, side=None,
                 side_rows=None, name):
    m, kdim = a.shape
    n = w.shape[2]
    ni, nj = m // tm, n // tn
    rows = kdim // ni
    row_blk = lambda j, i: jnp.where(j > 0, i, 0)
    col_blk = lambda j: jnp.maximum(j - 1, 0)
    blocks = [((tm, kdim), a.dtype), ((rows, tn), w.dtype), ((tm, tn), out_dtype)]
    scratch_blocks = [((2, kdim, tn), BF16), ((tm, tn), F32)]
    in_specs = [pl.BlockSpec((tm, kdim), lambda j, i: (row_blk(j, i), 0)),
                pl.BlockSpec((None, rows, tn),
                             lambda j, i: (layer, jnp.where(j < nj, i, ni - 1),
                                           jnp.minimum(j, nj - 1)))]
    out_specs = pl.BlockSpec((tm, tn), lambda j, i: (row_blk(j, i), col_blk(j)))
    out_shape = jax.ShapeDtypeStruct((m, n), out_dtype)
    args = [a, w]
    if side is not None:
        _, r, c = side.shape
        n_side = r // side_rows
        assert n_side * side_rows == r and n_side <= (nj + 1) * ni
        side_blk = lambda j, i: jnp.minimum(j * ni + i, n_side - 1)
        in_specs.append(pl.BlockSpec((None, side_rows, c), lambda j, i: (layer, side_blk(j, i), 0)))
        out_specs = [out_specs, pl.BlockSpec((side_rows, c), lambda j, i: (side_blk(j, i), 0))]
        out_shape = [out_shape, jax.ShapeDtypeStruct((r, c), BF16)]
        blocks += [((side_rows, c), F32), ((side_rows, c), BF16)]
        args.append(side)
    return pl.pallas_call(
        functools.partial(_mm_cast_kernel, epilogue=epilogue, has_side=side is not None),
        grid=(nj + 1, ni),
        in_specs=in_specs,
        out_specs=out_specs,
        out_shape=out_shape,
        scratch_shapes=[pltpu.VMEM((2, kdim, tn), BF16)],
        compiler_params=_params(("arbitrary", "arbitrary"), blocks, scratch_blocks),
        name=name,
    )(*args)


def _rope_kernel(pos_ref, inv_ref, cos_ref, sin_ref):
    ang = pos_ref[...].astype(F32) * inv_ref[...]
    cos_ref[...] = jnp.cos(ang)
    sin_ref[...] = jnp.sin(ang)


def _rope_tables(positions, *, tm=1024):
    t = positions.size
    half = RET_HEAD_DIM // 2
    inv = ROPE_BASE ** (-np.arange(0, RET_HEAD_DIM, 2, dtype=np.float64) / RET_HEAD_DIM)
    inv = jnp.asarray(inv.astype(np.float32)).reshape(1, half)
    blocks = [((tm, 1), jnp.int32), ((1, half), F32), ((tm, half), F32), ((tm, half), F32)]
    return pl.pallas_call(
        _rope_kernel,
        grid=(t // tm,),
        in_specs=[pl.BlockSpec((tm, 1), lambda i: (i, 0)),
                  pl.BlockSpec((1, half), lambda i: (0, 0))],
        out_specs=[pl.BlockSpec((tm, half), lambda i: (i, 0))] * 2,
        out_shape=[jax.ShapeDtypeStruct((t, half), F32)] * 2,
        compiler_params=_params(("parallel",), blocks),
        name="rope_tables",
    )(positions.reshape(t, 1), inv)


def _s5_prep_kernel(lr_ref, li_ref, ldt_ref, br_ref, bi_ref, cr_ref, ci_ref,
                    bblk_ref, cblk_ref, pn_ref, pp_ref):
    w = S5_TILE_STATES
    lr = lr_ref[...]
    li = li_ref[...]
    dt = jnp.exp(ldt_ref[...])
    a = lr * dt
    th = li * dt
    mag = jnp.exp(a)
    lbr = mag * jnp.cos(th)
    lbi = mag * jnp.sin(th)
    den = lr * lr + li * li
    nr = lbr - 1.0
    coef_r = (nr * lr + lbi * li) / den
    coef_i = (lbi * lr - nr * li) / den
    br = br_ref[...]
    bi = bi_ref[...]
    bbar_r = coef_r * br - coef_i * bi
    bbar_i = coef_r * bi + coef_i * br

    rows = lax.broadcasted_iota(jnp.int32, (V7X_LANES, w), 0)
    cols = lax.broadcasted_iota(jnp.int32, (V7X_LANES, w), 1)
    same_group = ((rows >> (SSM_GROUP.bit_length() - 1))
                  == (cols >> (SSM_STATE.bit_length() - 1)))
    reps = (S5_GROUPS_PER_TILE, 1)

    def blockdiag(v):
        return jnp.where(same_group, jnp.tile(v, reps), 0.0)

    bblk_ref[:, :w] = blockdiag(bbar_r).astype(bblk_ref.dtype)
    bblk_ref[:, w:] = blockdiag(bbar_i).astype(bblk_ref.dtype)
    cblk_ref[:, :w] = blockdiag(cr_ref[...]).astype(cblk_ref.dtype)
    cblk_ref[:, w:] = blockdiag(-ci_ref[...]).astype(cblk_ref.dtype)

    blk = S5_POW_BLOCK
    lo_m = (lax.broadcasted_iota(jnp.int32, (blk, w), 0) + 1).astype(F32)
    hi_m = (lax.broadcasted_iota(jnp.int32, (S5_CHUNK // blk, w), 0) * blk).astype(F32)

    def powers(m, sign):
        mag_m = jnp.exp(sign * (m * a))
        return mag_m * jnp.cos(m * th), sign * (mag_m * jnp.sin(m * th))

    for sign, ref in ((1.0, pp_ref), (-1.0, pn_ref)):
        lo_r, lo_i = powers(lo_m, sign)
        hi_r, hi_i = powers(hi_m, sign)
        for q in range(S5_CHUNK // blk):
            h_re, h_im = hi_r[q:q + 1, :], hi_i[q:q + 1, :]
            ref[q * blk:(q + 1) * blk, :w] = h_re * lo_r - h_im * lo_i
            ref[q * blk:(q + 1) * blk, w:] = h_re * lo_i + h_im * lo_r


def _s5_prep(lam_re, lam_im, log_dt, b_re, b_im, c_re, c_im):
    depth, g, p, h = b_re.shape
    gl, w = S5_GROUPS_PER_TILE, S5_TILE_STATES
    nt = depth * g // gl
    row = lambda v: v.reshape(nt, 1, w)
    bt = lambda v: v.reshape(nt, gl, p, h).transpose(0, 3, 1, 2).reshape(nt, h, w)
    ct = lambda v: v.reshape(nt, gl, h, p).transpose(0, 2, 1, 3).reshape(nt, h, w)
    row_spec = pl.BlockSpec((None, 1, w), lambda j: (j, 0, 0))
    mat_spec = pl.BlockSpec((None, h, w), lambda j: (j, 0, 0))
    blk_spec = pl.BlockSpec((None, V7X_LANES, 2 * w), lambda j: (j, 0, 0))
    pow_spec = pl.BlockSpec((None, S5_CHUNK, 2 * w), lambda j: (j, 0, 0))
    blocks = ([((1, w), F32)] * 3 + [((h, w), F32)] * 4 + [((V7X_LANES, 2 * w), BF16)] * 2
              + [((S5_CHUNK, 2 * w), F32)] * 2)
    return pl.pallas_call(
        _s5_prep_kernel,
        grid=(nt,),
        in_specs=[row_spec] * 3 + [mat_spec] * 4,
        out_specs=[blk_spec, blk_spec, pow_spec, pow_spec],
        out_shape=[jax.ShapeDtypeStruct((nt, V7X_LANES, 2 * w), BF16)] * 2
        + [jax.ShapeDtypeStruct((nt, S5_CHUNK, 2 * w), F32)] * 2,
        compiler_params=_params(("parallel",), blocks),
        name="s5_prep",
    )(row(lam_re), row(lam_im), row(jnp.repeat(log_dt, p)), bt(b_re), bt(b_im), ct(c_re), ct(c_im))


def _s5_kernel(u_ref, bblk_ref, cblk_ref, pn_ref, pp_ref, d_ref, y_ref, *, n_chunks):
    w = S5_TILE_STATES
    lc = S5_CHUNK
    bblk = bblk_ref[...]
    cblk = cblk_ref[...]
    d = d_ref[...]
    r = lax.broadcasted_iota(jnp.int32, (lc, lc), 0)
    c = lax.broadcasted_iota(jnp.int32, (lc, lc), 1)
    tri = (r >= c).astype(BF16)

    def chunk(t, carry):
        cr, ci = carry
        rows = pl.ds(pl.multiple_of(t * lc, lc), lc)
        u = u_ref[rows, :]
        bu = jnp.dot(u.astype(BF16), bblk, preferred_element_type=F32)
        br, bi = bu[:, :w], bu[:, w:]
        pnr, pni = pn_ref[:, :w], pn_ref[:, w:]
        z = jnp.concatenate([pnr * br - pni * bi, pnr * bi + pni * br], axis=1)
        ws = jnp.dot(tri, z.astype(BF16), preferred_element_type=F32)
        wr = ws[:, :w] + cr
        wi = ws[:, w:] + ci
        ppr, ppi = pp_ref[:, :w], pp_ref[:, w:]
        xr = ppr * wr - ppi * wi
        xi = ppr * wi + ppi * wr
        x = jnp.concatenate([xr, xi], axis=1).astype(BF16)
        y = lax.dot_general(x, cblk, (((1,), (1,)), ((), ())), preferred_element_type=F32)
        y_ref[rows, :] = jax.nn.gelu(y + d * u)
        return xr[lc - 1:, :], xi[lc - 1:, :]

    zero = jnp.zeros((1, w), F32)
    lax.fori_loop(0, n_chunks, chunk, (zero, zero), unroll=16)


def _s5_scan(proj, tables, layer, d_skip, bsz, seq):
    bblk, cblk, pn, pp = tables
    nt = SSM_WIDTH // V7X_LANES
    w2 = 2 * S5_TILE_STATES
    tab = lambda rows: pl.BlockSpec((None, rows, w2), lambda j, b: (layer * nt + j, 0, 0))
    blocks = [((seq, V7X_LANES), F32), ((V7X_LANES, w2), BF16), ((V7X_LANES, w2), BF16),
              ((S5_CHUNK, w2), F32), ((S5_CHUNK, w2), F32), ((1, V7X_LANES), F32),
              ((seq, V7X_LANES), F32)]
    return pl.pallas_call(
        functools.partial(_s5_kernel, n_chunks=seq // S5_CHUNK),
        grid=(nt, bsz),
        in_specs=[pl.BlockSpec((seq, V7X_LANES), lambda j, b: (b, j)),
                  tab(V7X_LANES), tab(V7X_LANES), tab(S5_CHUNK), tab(S5_CHUNK),
                  pl.BlockSpec((1, V7X_LANES), lambda j, b: (0, j))],
        out_specs=pl.BlockSpec((seq, V7X_LANES), lambda j, b: (b, j)),
        out_shape=jax.ShapeDtypeStruct((bsz * seq, SSM_WIDTH), F32),
        compiler_params=_params(("parallel", "parallel"), blocks),
        name="s5_scan",
    )(proj, bblk, cblk, pn, pp, d_skip.reshape(1, SSM_WIDTH))


def _glu_kernel(y_ref, w_ref, b_ref, g_ref, o_ref):
    y = y_ref[...]
    z = jnp.dot(y.astype(BF16), w_ref[...], preferred_element_type=F32) + b_ref[...]
    v = y * jax.nn.sigmoid(z)
    ms = jnp.mean(v * v, axis=-1, keepdims=True)
    o_ref[...] = (v * lax.rsqrt(ms + EPS) * g_ref[...]).astype(o_ref.dtype)


def _glu_norm(y, w_glu, layer, b_glu, g_out, *, tm=512):
    t, n = y.shape
    blocks = [((tm, n), F32), ((n, n), BF16), ((1, n), F32), ((1, n), F32), ((tm, n), BF16)]
    vec = pl.BlockSpec((1, n), lambda i: (0, 0))
    return pl.pallas_call(
        _glu_kernel,
        grid=(t // tm,),
        in_specs=[pl.BlockSpec((tm, n), lambda i: (i, 0)),
                  pl.BlockSpec((None, n, n), lambda i: (layer, 0, 0)), vec, vec],
        out_specs=pl.BlockSpec((tm, n), lambda i: (i, 0)),
        out_shape=jax.ShapeDtypeStruct((t, n), BF16),
        compiler_params=_params(("parallel",), blocks),
        name="glu_norm",
    )(y, w_glu, b_glu.reshape(1, n), g_out.reshape(1, n))


def _retention_tables():
    h, c, dh = RET_HEADS, RET_CHUNK, RET_HEAD_DIM
    log_gamma = np.log1p(-np.exp2(-5.0 - np.arange(h, dtype=np.float64)))
    idx = np.arange(c, dtype=np.float64)
    rel = idx[:, None] - idx[None, :]
    mask = np.where(rel >= 0, np.exp(log_gamma[:, None, None] * np.maximum(rel, 0.0)), 0.0)
    cross = np.exp(log_gamma[:, None] * (idx + 1.0))
    state = np.exp(log_gamma[:, None] * (c - 1.0 - idx))
    chunk = np.exp(log_gamma * c)
    k_scale = dh ** -0.5
    assert np.log2(k_scale) == np.round(np.log2(k_scale))
    mask = mask * k_scale
    state = state * k_scale
    wide = lambda v: np.broadcast_to(v[:, :, None], (h, c, dh))
    f = lambda v: jnp.asarray(np.ascontiguousarray(v).astype(np.float32))
    return f(mask), f(wide(cross)), f(wide(state)), f(np.broadcast_to(chunk[:, None, None], (h, 1, dh)))


def _retention_kernel(q_ref, k_ref, v_ref, g_ref, cos_ref, sin_ref, mask_ref, cd_ref, sd_ref,
                      ch_ref, gn_ref, o_ref, state_ref, *, chunks_per_step, heads_per_step):
    half = RET_HEAD_DIM // 2
    dh = RET_HEAD_DIM
    c = RET_CHUNK
    head0 = pl.program_id(2) * heads_per_step

    @pl.when(pl.program_id(1) == 0)
    def _():
        for e in range(heads_per_step):
            state_ref[head0 + e] = jnp.zeros(state_ref.shape[1:], F32)

    def rotate(t, cos, sin):
        t1, t2 = t[:, :half], t[:, half:]
        return jnp.concatenate([t1 * cos - t2 * sin, t1 * sin + t2 * cos], axis=1)

    for ci in range(chunks_per_step):
        rows = pl.ds(ci * c, c)
        cos = cos_ref[rows, :]
        sin = sin_ref[rows, :]
        for e in range(heads_per_step):
            hh = head0 + e
            cols = slice(e * dh, (e + 1) * dh)
            q = rotate(q_ref[rows, cols], cos, sin)
            k = rotate(k_ref[rows, cols], cos, sin)
            v = v_ref[rows, cols].astype(BF16)
            qb = q.astype(BF16)
            state = state_ref[hh]
            scores = lax.dot_general(qb, k.astype(BF16), (((1,), (1,)), ((), ())),
                                     preferred_element_type=F32) * mask_ref[hh]
            inner = jnp.dot(scores.astype(BF16), v, preferred_element_type=F32)
            cross = jnp.dot(qb, state.astype(BF16), preferred_element_type=F32) * cd_ref[hh]
            kv = lax.dot_general((k * sd_ref[hh]).astype(BF16), v, (((0,), (0,)), ((), ())),
                                 preferred_element_type=F32)
            state_ref[hh] = state * ch_ref[hh] + kv
            out = inner + cross
            mu = jnp.mean(out, axis=-1, keepdims=True)
            dev = out - mu
            var = jnp.mean(dev * dev, axis=-1, keepdims=True)
            g = g_ref[rows, cols]
            o_ref[rows, cols] = (dev * lax.rsqrt(var + EPS) * gn_ref[hh]
                                 * (g * jax.nn.sigmoid(g))).astype(o_ref.dtype)


def _retention(proj, cos, sin, g_norm, bsz, seq, *, chunks_per_step=8, heads_per_step=4):
    dh, h = RET_HEAD_DIM, RET_HEADS
    rb = chunks_per_step * RET_CHUNK
    steps = seq // rb
    wide = heads_per_step * dh
    groups = h // heads_per_step
    mask, cd, sd, ch = _retention_tables()
    first = SSM_WIDTH // wide
    col = lambda which: pl.BlockSpec(
        (rb, wide), lambda b, s, hg, which=which: (b * steps + s, first + which * groups + hg))
    rope = pl.BlockSpec((rb, dh // 2), lambda b, s, hg: (b * steps + s, 0))
    whole = lambda rows, cols: pl.BlockSpec((h, rows, cols), lambda b, s, hg: (0, 0, 0))
    blocks = ([((rb, wide), F32)] * 4 + [((rb, dh // 2), F32)] * 2
              + [((h, RET_CHUNK, RET_CHUNK), F32), ((h, RET_CHUNK, dh), F32),
                 ((h, RET_CHUNK, dh), F32), ((h, 1, dh), F32), ((h, 1, dh), F32), ((rb, wide), BF16)])
    return pl.pallas_call(
        functools.partial(_retention_kernel, chunks_per_step=chunks_per_step,
                          heads_per_step=heads_per_step),
        grid=(bsz, steps, groups),
        in_specs=[col(0), col(1), col(2), col(3), rope, rope,
                  whole(RET_CHUNK, RET_CHUNK), whole(RET_CHUNK, dh), whole(RET_CHUNK, dh),
                  whole(1, dh), whole(1, dh)],
        out_specs=pl.BlockSpec((rb, wide), lambda b, s, hg: (b * steps + s, hg)),
        out_shape=jax.ShapeDtypeStruct((bsz * seq, RET_WIDTH), BF16),
        scratch_shapes=[pltpu.VMEM((h, dh, dh), F32)],
        compiler_params=_params(("parallel", "arbitrary", "arbitrary"), blocks,
                                [((h, dh, dh), F32)]),
        name="retention",
    )(proj, proj, proj, proj, cos, sin, mask, cd, sd, ch, g_norm.reshape(h, 1, dh))


def kernel(x, c, positions, w_ada, b_ada, g_mix, w_in, lam_re, lam_im, log_dt, b_re, b_im, c_re,
           c_im, d_skip, w_glu, b_glu, g_ssm_out, g_ret_norm, w_out, g_mlp, w_up, w_down, g_final):
    bsz, seq, d = x.shape
    depth = w_ada.shape[0]
    x2 = x.reshape(bsz * seq, d)
    c_pad = jnp.zeros((ADA_ROWS, d), F32).at[:bsz].set(c)
    b_ada3 = b_ada.reshape(depth, 1, b_ada.shape[1])
    mod = _ada_mod(c_pad, w_ada, b_ada3, 0)
    cos, sin = _rope_tables(positions)
    tables = _s5_prep(lam_re, lam_im, log_dt, b_re, b_im, c_re, c_im)
    w_glu_b = w_glu.astype(BF16)
    x_norm_in = x2
    for i in range(depth):
        shift1, scale1, gate1, shift2, scale2, gate2 = jnp.split(mod[:bsz], 6, axis=-1)

        h = _normmod(x_norm_in, g_mix[i], scale1, shift1, seq)
        proj, w_out_b = _matmul_cast(h, w_in, i, tm=1024, tn=1024, out_dtype=F32,
                                     side=w_out, side_rows=64, name="in_proj")
        y = _s5_scan(proj, tables, i, d_skip[i], bsz, seq)
        ssm_out = _glu_norm(y, w_glu_b, i, b_glu[i], g_ssm_out[i])
        ret_out = _retention(proj, cos, sin, g_ret_norm[i], bsz, seq)
        x2, x_norm_in = _matmul([ssm_out, ret_out], w_out_b[None], 0, tm=1024, tn=1024, tk=d,
                                out_dtype=F32, epilogue=_epi_resid, resid=x2, gate=gate1, seq=seq,
                                copy_dtype=BF16, name="out_proj")

        h = _normmod(x_norm_in, g_mlp[i], scale2, shift2, seq)
        a, w_down_b = _matmul_cast(h, w_up, i, tm=1024, tn=1024, out_dtype=BF16,
                                   epilogue=_epi_relu2, side=w_down, side_rows=128, name="mlp_up")
        more = i + 1 < depth
        res = _matmul([a], w_down_b[None], 0, tm=1024, tn=1024, tk=2048, out_dtype=F32,
                      epilogue=_epi_resid, resid=x2, gate=gate2, seq=seq,
                      ada=(c_pad, w_ada, b_ada3, i + 1) if more else None,
                      copy_dtype=BF16 if more else None, name="mlp_down")
        x2, mod, x_norm_in = res if more else (res, None, None)
    return _rmsnorm(x2, g_final).reshape(bsz, seq, d)
```
